```python
import math
import jax, jax.numpy as jnp
from jax import lax
import numpy as np

D_MODEL = 4096
BATCH = 1
SEQ = 16384
DEPTH = 1

N_META = 16
BLOCK_Q = 128
LEAD_PAD = BLOCK_Q - N_META
MIX_WIDTH = D_MODEL
ATTN_WIDTH = MIX_WIDTH // 2
CONV_WIDTH = MIX_WIDTH - ATTN_WIDTH
N_DIFF_HEADS = 8
N_SOFTMAX_MAPS = 2 * N_DIFF_HEADS
DK = ATTN_WIDTH // N_SOFTMAX_MAPS
DV = 2 * DK
IN_WIDTH = 3 * ATTN_WIDTH + 2 * CONV_WIDTH
CONV_KERNEL = 31
D_FF = 4 * D_MODEL
REL_BUCKETS = 32
REL_MAX_EXACT = REL_BUCKETS // 2
REL_MAX_DISTANCE = 128
NORM_EPS = 1e-6
NEG_INF = -1e30

kernel_name = "hymba_diffattn_conformer_conv_hybrid"


def rmsnorm(x, g):
    xf = x.astype(jnp.float32)
    y = xf * lax.rsqrt(jnp.mean(xf * xf, axis=-1, keepdims=True) + NORM_EPS)
    return (y * g.astype(jnp.float32)).astype(x.dtype)


def layernorm(x, g, b):
    xf = x.astype(jnp.float32)
    mu = jnp.mean(xf, axis=-1, keepdims=True)
    var = jnp.mean(jnp.square(xf - mu), axis=-1, keepdims=True)
    y = (xf - mu) * lax.rsqrt(var + NORM_EPS) * g.astype(jnp.float32) + b.astype(jnp.float32)
    return y.astype(x.dtype)


def t5_causal_bucket(dist):
    n = jnp.maximum(dist, 0)
    nf = jnp.maximum(n, 1).astype(jnp.float32)
    large = REL_MAX_EXACT + (jnp.log(nf / REL_MAX_EXACT)
                             / math.log(REL_MAX_DISTANCE / REL_MAX_EXACT)
                             * (REL_BUCKETS - REL_MAX_EXACT)).astype(jnp.int32)
    large = jnp.minimum(large, REL_BUCKETS - 1)
    return jnp.where(n < REL_MAX_EXACT, n, large)


def diff_attention(q, k, v, rel_bias, lam, lambda_init, subln_w):
    B, LP = q.shape[0], q.shape[1]
    nblk = LP // BLOCK_Q
    scale = DK ** -0.5
    qb = q.reshape(B, nblk, BLOCK_Q, N_SOFTMAX_MAPS, DK).transpose(1, 0, 2, 3, 4)
    starts = jnp.arange(nblk, dtype=jnp.int32) * BLOCK_Q
    kpos = jnp.arange(LP, dtype=jnp.int32)
    key_valid = kpos >= LEAD_PAD

    def block(args):
        qblk, start = args
        qpos = start + jnp.arange(BLOCK_Q, dtype=jnp.int32)
        dist = qpos[:, None] - kpos[None, :]
        bias = jnp.transpose(rel_bias[t5_causal_bucket(dist)], (2, 0, 1)).astype(jnp.float32)
        mask = (dist >= 0) & key_valid[None, :]
        s = jnp.einsum('bqhd,bkhd->bhqk', qblk, k).astype(jnp.float32) * scale + bias[None]
        s = jnp.where(mask[None, None], s, NEG_INF)
        p = jax.nn.softmax(s, axis=-1).reshape(B, N_DIFF_HEADS, 2, BLOCK_Q, LP)
        a = p[:, :, 0] - lam * p[:, :, 1]
        return jnp.einsum('bhqk,bkhe->bqhe', a.astype(v.dtype), v)

    out = lax.map(block, (qb, starts))
    out = out.transpose(1, 0, 2, 3, 4).reshape(B, LP, N_DIFF_HEADS, DV)
    out = rmsnorm(out, subln_w) * (1.0 - lambda_init)
    return out.reshape(B, LP, N_DIFF_HEADS * DV)


def conformer_conv(u, valid, conv_w, conv_b, ln_g, ln_b):
    a, gate = jnp.split(u, 2, axis=-1)
    g = a * jax.nn.sigmoid(gate) * valid[None, :, None]
    y = lax.conv_general_dilated(
        g, conv_w[:, None, :].astype(g.dtype), window_strides=(1,),
        padding=[(CONV_KERNEL - 1, 0)],
        dimension_numbers=('NWC', 'WIO', 'NWC'),
        feature_group_count=CONV_WIDTH) + conv_b
    y = layernorm(y, ln_g, ln_b)
    return jax.nn.silu(y)


def setup_inputs(seed: int = 0) -> dict:
    key = jax.random.key(seed)
    ks = jax.random.split(key, 20)
    f32 = jnp.float32
    n = lambda k, shape, s: jax.random.normal(k, shape, f32) * s
    return {
        "x": n(ks[0], (BATCH, SEQ, D_MODEL), 1.0),
        "meta_tokens": n(ks[1], (N_META, D_MODEL), 1.0),
        "rel_bias": n(ks[2], (REL_BUCKETS, N_SOFTMAX_MAPS), 0.5),
        "norm_mix": 1.0 + n(ks[3], (DEPTH, D_MODEL), 0.02),
        "w_in": n(ks[4], (DEPTH, D_MODEL, IN_WIDTH), D_MODEL ** -0.5),
        "lambda_q1": n(ks[5], (DEPTH, DK), 0.1),
        "lambda_k1": n(ks[6], (DEPTH, DK), 0.1),
        "lambda_q2": n(ks[7], (DEPTH, DK), 0.1),
        "lambda_k2": n(ks[8], (DEPTH, DK), 0.1),
        "subln_w": 1.0 + n(ks[9], (DEPTH, DV), 0.02),
        "conv_w": n(ks[10], (DEPTH, CONV_KERNEL, CONV_WIDTH), CONV_KERNEL ** -0.5),
        "conv_b": n(ks[11], (DEPTH, CONV_WIDTH), 0.01),
        "conv_ln_g": 1.0 + n(ks[12], (DEPTH, CONV_WIDTH), 0.02),
        "conv_ln_b": n(ks[13], (DEPTH, CONV_WIDTH), 0.01),
        "merge_scale": 1.0 + n(ks[14], (DEPTH, MIX_WIDTH), 0.02),
        "w_out": n(ks[15], (DEPTH, MIX_WIDTH, D_MODEL), MIX_WIDTH ** -0.5),
        "norm_mlp": 1.0 + n(ks[16], (DEPTH, D_MODEL), 0.02),
        "w_up": n(ks[17], (DEPTH, D_MODEL, D_FF), D_MODEL ** -0.5),
        "w_down": n(ks[18], (DEPTH, D_FF, D_MODEL), D_FF ** -0.5),
        "norm_final": 1.0 + n(ks[19], (D_MODEL,), 0.02),
    }


def reference(x, meta_tokens, rel_bias, norm_mix, w_in, lambda_q1, lambda_k1,
              lambda_q2, lambda_k2, subln_w, conv_w, conv_b, conv_ln_g,
              conv_ln_b, merge_scale, w_out, norm_mlp, w_up, w_down, norm_final):
    B = x.shape[0]
    pad = jnp.zeros((B, LEAD_PAD, D_MODEL), x.dtype)
    meta = jnp.broadcast_to(meta_tokens[None].astype(x.dtype), (B, N_META, D_MODEL))
    h = jnp.concatenate([pad, meta, x], axis=1)
    LP = h.shape[1]
    valid = (jnp.arange(LP) >= LEAD_PAD).astype(h.dtype)

    for l in range(DEPTH):
        lambda_init = 0.8 - 0.6 * math.exp(-0.3 * l)
        xn = rmsnorm(h, norm_mix[l])
        proj = jnp.einsum('bld,de->ble', xn, w_in[l])
        q, k, v, u = jnp.split(proj, [ATTN_WIDTH, 2 * ATTN_WIDTH, 3 * ATTN_WIDTH], axis=-1)
        q = q.reshape(B, LP, N_SOFTMAX_MAPS, DK)
        k = k.reshape(B, LP, N_SOFTMAX_MAPS, DK)
        v = v.reshape(B, LP, N_DIFF_HEADS, DV)
        lam = (jnp.exp(jnp.sum(lambda_q1[l].astype(jnp.float32) * lambda_k1[l].astype(jnp.float32)))
               - jnp.exp(jnp.sum(lambda_q2[l].astype(jnp.float32) * lambda_k2[l].astype(jnp.float32)))
               + lambda_init)
        attn = diff_attention(q, k, v, rel_bias, lam, lambda_init, subln_w[l])
        conv = conformer_conv(u, valid, conv_w[l], conv_b[l], conv_ln_g[l], conv_ln_b[l])
        mixed = jnp.concatenate([attn, conv], axis=-1) * merge_scale[l]
        h = h + jnp.einsum('blm,md->bld', mixed, w_out[l])
        hn = rmsnorm(h, norm_mlp[l])
        hid = jnp.square(jax.nn.relu(jnp.einsum('bld,df->blf', hn, w_up[l])))
        h = h + jnp.einsum('blf,fd->bld', hid, w_down[l])

    out = rmsnorm(h, norm_final)
    return out[:, LEAD_PAD + N_META:]
```

```python
import functools
import math

import jax
import jax.numpy as jnp
from jax import lax
from jax.experimental import pallas as pl
from jax.experimental.pallas import tpu as pltpu

F32 = jnp.float32
BF16 = jnp.bfloat16

N_META = 16
N_DIFF_HEADS = 8
N_SOFTMAX_MAPS = 2 * N_DIFF_HEADS
DK = 128
DV = 2 * DK
CONV_KERNEL = 31
REL_BUCKETS = 32
REL_MAX_EXACT = REL_BUCKETS // 2
REL_MAX_DISTANCE = 128
NORM_EPS = 1e-6
NEG_INF = -1e30
LAMBDA_INIT = 0.8 - 0.6 * math.exp(-0.3 * 0)

LANES = 128
SUBLANES = 8
META_PAD = LANES
VMEM_LIMIT = 56 * 1024 * 1024

ATTN_TQ = 512
ATTN_TK = 512
CONV_HALO = 32
CONV_RB = 64


def _params(*sem):
    return pltpu.CompilerParams(dimension_semantics=sem, vmem_limit_bytes=VMEM_LIMIT)


def _rmsnorm_kernel(x_ref, g_ref, o_ref):
    x = x_ref[...]
    y = x * lax.rsqrt(jnp.mean(x * x, axis=-1, keepdims=True) + NORM_EPS)
    o_ref[...] = (y * g_ref[...]).astype(o_ref.dtype)


def _rmsnorm(x, g, out_dtype, rows):
    m, d = x.shape
    return pl.pallas_call(
        _rmsnorm_kernel,
        grid=(m // rows,),
        in_specs=[pl.BlockSpec((rows, d), lambda i: (i, 0)),
                  pl.BlockSpec((1, d), lambda i: (0, 0))],
        out_specs=pl.BlockSpec((rows, d), lambda i: (i, 0)),
        out_shape=jax.ShapeDtypeStruct((m, d), out_dtype),
        compiler_params=_params("parallel"),
        name="rmsnorm",
    )(x, g.reshape(1, d))


def _mm_kernel(a_ref, w_ref, o_ref):
    o_ref[...] = jnp.dot(a_ref[...], w_ref[...], preferred_element_type=F32).astype(o_ref.dtype)


def _qkv_proj(a, w, n_cols, tm, tn):
    m, k = a.shape
    return pl.pallas_call(
        _mm_kernel,
        grid=(m // tm, n_cols // tn),
        in_specs=[pl.BlockSpec((tm, k), lambda i, j: (i, 0)),
                  pl.BlockSpec((k, tn), lambda i, j: (0, j))],
        out_specs=pl.BlockSpec((tm, tn), lambda i, j: (i, j)),
        out_shape=jax.ShapeDtypeStruct((m, n_cols), BF16),
        compiler_params=_params("parallel", "arbitrary"),
        name="qkv_proj",
    )(a, w)


def _glu_kernel(a_ref, wa_ref, wg_ref, o_ref):
    a = a_ref[...]
    val = jnp.dot(a, wa_ref[...], preferred_element_type=F32)
    gate = jnp.dot(a, wg_ref[...], preferred_element_type=F32)
    o_ref[...] = val * jax.nn.sigmoid(gate)


def _glu_proj(a, w, col0, width, tm, tn):
    m, k = a.shape
    off_a, off_g = col0 // tn, (col0 + width) // tn
    return pl.pallas_call(
        _glu_kernel,
        grid=(m // tm, width // tn),
        in_specs=[pl.BlockSpec((tm, k), lambda i, j: (i, 0)),
                  pl.BlockSpec((k, tn), lambda i, j: (0, off_a + j)),
                  pl.BlockSpec((k, tn), lambda i, j: (0, off_g + j))],
        out_specs=pl.BlockSpec((tm, tn), lambda i, j: (i, j)),
        out_shape=jax.ShapeDtypeStruct((m, width), F32),
        compiler_params=_params("parallel", "arbitrary"),
        name="glu_proj",
    )(a, w, w)


def _outproj_kernel(attn_ref, conv_ref, w1_ref, w2_ref, x_ref, o_ref):
    acc = jnp.dot(attn_ref[...], w1_ref[...], preferred_element_type=F32)
    acc = acc + jnp.dot(conv_ref[...], w2_ref[...], preferred_element_type=F32)
    o_ref[...] = x_ref[...] + acc


def _out_proj(attn, conv, w, x, tm, tn):
    m, ka = attn.shape
    kc = conv.shape[1]
    n = w.shape[1]
    assert ka == kc
    return pl.pallas_call(
        _outproj_kernel,
        grid=(m // tm, n // tn),
        in_specs=[pl.BlockSpec((tm, ka), lambda i, j: (i, 0)),
                  pl.BlockSpec((tm, kc), lambda i, j: (i, 0)),
                  pl.BlockSpec((ka, tn), lambda i, j: (0, j)),
                  pl.BlockSpec((kc, tn), lambda i, j: (1, j)),
                  pl.BlockSpec((tm, tn), lambda i, j: (i, j))],
        out_specs=pl.BlockSpec((tm, tn), lambda i, j: (i, j)),
        out_shape=jax.ShapeDtypeStruct((m, n), F32),
        compiler_params=_params("parallel", "arbitrary"),
        name="out_proj",
    )(attn, conv, w, w, x)


def _up_kernel(a_ref, w_ref, o_ref):
    h = jnp.maximum(jnp.dot(a_ref[...], w_ref[...], preferred_element_type=F32), 0.0)
    o_ref[...] = (h * h).astype(o_ref.dtype)


def _mlp_up(a, w, tm, tn):
    m, k = a.shape
    n = w.shape[1]
    return pl.pallas_call(
        _up_kernel,
        grid=(m // tm, n // tn),
        in_specs=[pl.BlockSpec((tm, k), lambda i, j: (i, 0)),
                  pl.BlockSpec((k, tn), lambda i, j: (0, j))],
        out_specs=pl.BlockSpec((tm, tn), lambda i, j: (i, j)),
        out_shape=jax.ShapeDtypeStruct((m, n), BF16),
        compiler_params=_params("parallel", "arbitrary"),
        name="mlp_up",
    )(a, w)


def _down_kernel(a_ref, w_ref, h_ref, o_ref):
    part = jnp.dot(a_ref[...], w_ref[...], preferred_element_type=F32)

    @pl.when(pl.program_id(2) == 0)
    def _():
        o_ref[...] = h_ref[...] + part

    @pl.when(pl.program_id(2) > 0)
    def _():
        o_ref[...] += part


def _mlp_down(a, w, h, tm, tn, tk):
    m, k = a.shape
    n = w.shape[1]
    return pl.pallas_call(
        _down_kernel,
        grid=(m // tm, n // tn, k // tk),
        in_specs=[pl.BlockSpec((tm, tk), lambda i, j, kk: (i, kk)),
                  pl.BlockSpec((tk, tn), lambda i, j, kk: (kk, j)),
                  pl.BlockSpec((tm, tn), lambda i, j, kk: (i, j))],
        out_specs=pl.BlockSpec((tm, tn), lambda i, j, kk: (i, j)),
        out_shape=jax.ShapeDtypeStruct((m, n), F32),
        compiler_params=_params("parallel", "parallel", "arbitrary"),
        name="mlp_down",
    )(a, w, h)


def _bias_kernel(rb_ref, near_ref, meta_ref, *, tq, tk):
    h = pl.program_id(0)
    far = rb_ref[REL_BUCKETS - 1, h]

    def table(dist):
        n = jnp.maximum(dist, 0)
        nf = jnp.maximum(n, 1).astype(F32)
        large = REL_MAX_EXACT + (jnp.log(nf / REL_MAX_EXACT)
                                 / math.log(REL_MAX_DISTANCE / REL_MAX_EXACT)
                                 * (REL_BUCKETS - REL_MAX_EXACT)).astype(jnp.int32)
        large = jnp.minimum(large, REL_BUCKETS - 1)
        bucket = jnp.where(n < REL_MAX_EXACT, n, large)
        out = jnp.zeros(dist.shape, F32)
        for b in range(REL_BUCKETS):
            out = jnp.where(bucket == b, rb_ref[b, h], out)
        return out - far

    r = lax.broadcasted_iota(jnp.int32, (tq, tk), 0)
    c = lax.broadcasted_iota(jnp.int32, (tq, tk), 1)
    near_ref[0, 0] = jnp.where(r >= c, table(r - c), NEG_INF)
    near_ref[0, 1] = table(tk + r - c)
    rm = lax.broadcasted_iota(jnp.int32, (tq, META_PAD), 0)
    cm = lax.broadcasted_iota(jnp.int32, (tq, META_PAD), 1)
    valid = cm < N_META
    meta_ref[0, 0] = jnp.where(valid, table(N_META + rm - cm), NEG_INF)
    meta_ref[0, 1] = jnp.where(valid, 0.0, NEG_INF)


def _bias_tiles(rel_bias, tq, tk):
    assert tq == tk and tk + 1 >= REL_MAX_DISTANCE and tq + 1 >= REL_MAX_DISTANCE
    return pl.pallas_call(
        functools.partial(_bias_kernel, tq=tq, tk=tk),
        grid=(N_SOFTMAX_MAPS,),
        in_specs=[pl.BlockSpec(memory_space=pltpu.SMEM)],
        out_specs=[pl.BlockSpec((1, 2, tq, tk), lambda h: (h, 0, 0, 0)),
                   pl.BlockSpec((1, 2, tq, META_PAD), lambda h: (h, 0, 0, 0))],
        out_shape=[jax.ShapeDtypeStruct((N_SOFTMAX_MAPS, 2, tq, tk), F32),
                   jax.ShapeDtypeStruct((N_SOFTMAX_MAPS, 2, tq, META_PAD), F32)],
        compiler_params=_params("parallel"),
        name="bias_tiles",
    )(rel_bias)


def _attn_kernel(lq1_ref, lk1_ref, lq2_ref, lk2_ref, q_ref, k_ref, v_ref, km_ref, vm_ref,
                 near_ref, meta_ref, subln_ref, merge_ref, o_ref, m_sc, l_sc, acc_sc):
    qi = pl.program_id(1)
    kj = pl.program_id(2)
    scale = DK ** -0.5

    def scores(mp, key_ref):
        q = q_ref[:, mp * DK:(mp + 1) * DK]
        k = key_ref[:, mp * DK:(mp + 1) * DK]
        s = lax.dot_general(q, k, (((1,), (1,)), ((), ())), preferred_element_type=F32)
        return s * scale

    @pl.when(kj == 0)
    def _meta_keys():
        var = jnp.minimum(qi, 1)
        for mp in range(2):
            s = scores(mp, km_ref) + meta_ref[mp, var]
            m = jnp.max(s, axis=1, keepdims=True)
            p = jnp.exp(s - m)
            m_sc[mp] = m
            l_sc[mp] = jnp.sum(p, axis=1, keepdims=True)
            acc_sc[mp] = jnp.dot(p.astype(BF16), vm_ref[...], preferred_element_type=F32)

    def key_tile(with_bias):
        for mp in range(2):
            s = scores(mp, k_ref)
            if with_bias:
                s = s + near_ref[mp, jnp.where(kj == qi, 0, 1)]
            m_prev = m_sc[mp]
            m_new = jnp.maximum(m_prev, jnp.max(s, axis=1, keepdims=True))
            alpha = jnp.exp(m_prev - m_new)
            p = jnp.exp(s - m_new)
            l_sc[mp] = alpha * l_sc[mp] + jnp.sum(p, axis=1, keepdims=True)
            acc_sc[mp] = alpha * acc_sc[mp] + jnp.dot(p.astype(BF16), v_ref[...],
                                                      preferred_element_type=F32)
            m_sc[mp] = m_new

    @pl.when(jnp.logical_and(kj <= qi, kj >= qi - 1))
    def _near():
        key_tile(True)

    @pl.when(kj < qi - 1)
    def _far():
        key_tile(False)

    @pl.when(kj == qi)
    def _finish():
        lam = (jnp.exp(jnp.sum(lq1_ref[...] * lk1_ref[...], axis=-1, keepdims=True))
               - jnp.exp(jnp.sum(lq2_ref[...] * lk2_ref[...], axis=-1, keepdims=True))
               + LAMBDA_INIT)
        o = acc_sc[0] / l_sc[0] - lam * (acc_sc[1] / l_sc[1])
        y = o * lax.rsqrt(jnp.mean(o * o, axis=-1, keepdims=True) + NORM_EPS)
        y = (y * subln_ref[...]) * (1.0 - LAMBDA_INIT)
        o_ref[...] = (y * merge_ref[...]).astype(o_ref.dtype)


def _attention(qkv, qkv_meta, near, meta, lams, subln, merge, tq, tk):
    s = qkv.shape[0]
    nq = s // tq
    hq, hk, hv = 0, N_DIFF_HEADS, 2 * N_DIFF_HEADS
    vec = lambda: pl.BlockSpec((1, DK), lambda h, i, j: (0, 0))
    return pl.pallas_call(
        _attn_kernel,
        grid=(N_DIFF_HEADS, nq, nq),
        in_specs=[vec(), vec(), vec(), vec(),
                  pl.BlockSpec((tq, DV), lambda h, i, j: (i, hq + h)),
                  pl.BlockSpec((tk, DV), lambda h, i, j: (jnp.minimum(j, i), hk + h)),
                  pl.BlockSpec((tk, DV), lambda h, i, j: (jnp.minimum(j, i), hv + h)),
                  pl.BlockSpec((META_PAD, DV), lambda h, i, j: (0, hk + h)),
                  pl.BlockSpec((META_PAD, DV), lambda h, i, j: (0, hv + h)),
                  pl.BlockSpec((2, 2, tq, tk), lambda h, i, j: (h, 0, 0, 0)),
                  pl.BlockSpec((2, 2, tq, META_PAD), lambda h, i, j: (h, 0, 0, 0)),
                  pl.BlockSpec((1, DV), lambda h, i, j: (0, 0)),
                  pl.BlockSpec((1, DV), lambda h, i, j: (0, h))],
        out_specs=pl.BlockSpec((tq, DV), lambda h, i, j: (i, h)),
        out_shape=jax.ShapeDtypeStruct((s, N_DIFF_HEADS * DV), BF16),
        scratch_shapes=[pltpu.VMEM((2, tq, 1), F32),
                        pltpu.VMEM((2, tq, 1), F32),
                        pltpu.VMEM((2, tq, DV), F32)],
        compiler_params=_params("parallel", "parallel", "arbitrary"),
        name="diff_attention",
    )(*lams, qkv, qkv, qkv, qkv_meta, qkv_meta, near, meta, subln, merge)


def _conv_kernel(g_ref, prev_ref, first_ref, w_ref, b_ref, lng_ref, lnb_ref, merge_ref, o_ref,
                 ext_sc, y_sc, *, tm, c):
    i = pl.program_id(0)
    d0 = CONV_HALO - (CONV_KERNEL - 1)

    @pl.when(i == 0)
    def _():
        ext_sc[0:CONV_HALO, :] = first_ref[...]

    @pl.when(i > 0)
    def _():
        ext_sc[0:CONV_HALO, :] = prev_ref[...]

    ext_sc[CONV_HALO:CONV_HALO + tm, :] = g_ref[...]
    ext_sc[CONV_HALO + tm:CONV_HALO + tm + SUBLANES, :] = jnp.zeros((SUBLANES, c), F32)

    rb = CONV_RB
    n_load = rb + SUBLANES + CONV_HALO

    def lane_group(lg, carry):
        l0 = pl.multiple_of(lg * LANES, LANES)
        w = w_ref[:, pl.ds(l0, LANES)]
        bias = b_ref[:, pl.ds(l0, LANES)]

        def row_block(r, carry2):
            r0 = pl.multiple_of(r * rb, rb)
            e = ext_sc[pl.ds(r0, n_load), pl.ds(l0, LANES)]
            y = jnp.broadcast_to(bias, (rb, LANES))
            for s in range(SUBLANES):
                q = None
                for k in range((CONV_HALO + SUBLANES) // SUBLANES):
                    j = s + SUBLANES * k - d0
                    if 0 <= j < CONV_KERNEL:
                        term = w[j:j + 1, :] * e[SUBLANES * k:SUBLANES * k + rb + SUBLANES, :]
                        q = term if q is None else q + term
                y = y + q[s:s + rb, :]
            y_sc[pl.ds(r0, rb), pl.ds(l0, LANES)] = y
            return carry2

        return lax.fori_loop(0, tm // rb, row_block, carry)

    lax.fori_loop(0, c // LANES, lane_group, 0)

    def norm_block(r, carry):
        r0 = pl.multiple_of(r * SUBLANES, SUBLANES)
        y = y_sc[pl.ds(r0, SUBLANES), :]
        mu = jnp.mean(y, axis=-1, keepdims=True)
        var = jnp.mean(jnp.square(y - mu), axis=-1, keepdims=True)
        z = (y - mu) * lax.rsqrt(var + NORM_EPS) * lng_ref[...] + lnb_ref[...]
        z = z * jax.nn.sigmoid(z)
        o_ref[pl.ds(r0, SUBLANES), :] = (z * merge_ref[...]).astype(o_ref.dtype)
        return carry

    lax.fori_loop(0, tm // SUBLANES, norm_block, 0)


def _conv_branch(g, first_rows, conv_w, conv_b, ln_g, ln_b, merge, tm):
    s, c = g.shape
    assert tm % CONV_HALO == 0 and tm % CONV_RB == 0 and CONV_HALO >= CONV_KERNEL - 1
    halo_blocks = tm // CONV_HALO
    row = lambda: pl.BlockSpec((1, c), lambda i: (0, 0))
    return pl.pallas_call(
        functools.partial(_conv_kernel, tm=tm, c=c),
        grid=(s // tm,),
        in_specs=[pl.BlockSpec((tm, c), lambda i: (i, 0)),
                  pl.BlockSpec((CONV_HALO, c), lambda i: (jnp.maximum(i * halo_blocks - 1, 0), 0)),
                  pl.BlockSpec((CONV_HALO, c), lambda i: (0, 0)),
                  pl.BlockSpec((CONV_KERNEL, c), lambda i: (0, 0)),
                  row(), row(), row(),
                  pl.BlockSpec((1, c), lambda i: (0, 1))],
        out_specs=pl.BlockSpec((tm, c), lambda i: (i, 0)),
        out_shape=jax.ShapeDtypeStruct((s, c), BF16),
        scratch_shapes=[pltpu.VMEM((CONV_HALO + tm + SUBLANES, c), F32),
                        pltpu.VMEM((tm, c), F32)],
        compiler_params=_params("parallel"),
        name="conv_branch",
    )(g, g, first_rows, conv_w, conv_b.reshape(1, c), ln_g.reshape(1, c), ln_b.reshape(1, c), merge)


def kernel(x, meta_tokens, rel_bias, norm_mix, w_in, lambda_q1, lambda_k1, lambda_q2, lambda_k2,
           subln_w, conv_w, conv_b, conv_ln_g, conv_ln_b, merge_scale, w_out, norm_mlp, w_up,
           w_down, norm_final):
    batch, seq, d = x.shape
    assert batch == 1 and w_in.shape[0] == 1, "single sequence, single layer"
    attn_w = N_DIFF_HEADS * DV
    conv_c = conv_w.shape[-1]
    assert w_in.shape[-1] == 3 * attn_w + 2 * conv_c and meta_tokens.shape[0] == N_META
    d_ff = w_up.shape[-1]

    tm = min(1024, seq)
    x2 = x[0]
    w_in_b = w_in[0].astype(BF16)
    w_out_b = w_out[0].astype(BF16)
    w_up_b = w_up[0].astype(BF16)
    w_down_b = w_down[0].astype(BF16)
    merge = merge_scale[0].reshape(1, attn_w + conv_c)

    xn = _rmsnorm(x2, norm_mix[0], BF16, rows=256)
    xn_meta = _rmsnorm(meta_tokens, norm_mix[0], BF16, rows=N_META)
    qkv = _qkv_proj(xn, w_in_b, 3 * attn_w, tm, 1024)
    qkv_meta = _qkv_proj(xn_meta, w_in_b, 3 * attn_w, N_META, 1024)
    glu = _glu_proj(xn, w_in_b, 3 * attn_w, conv_c, tm, 512)
    glu_meta = _glu_proj(xn_meta, w_in_b, 3 * attn_w, conv_c, N_META, 512)

    near, meta_bias = _bias_tiles(rel_bias, ATTN_TQ, ATTN_TK)
    qkv_meta_pad = jnp.pad(qkv_meta, ((0, META_PAD - N_META), (0, 0)))
    lams = [v[0].reshape(1, DK) for v in (lambda_q1, lambda_k1, lambda_q2, lambda_k2)]
    attn = _attention(qkv, qkv_meta_pad, near, meta_bias, lams, subln_w[0].reshape(1, DV), merge,
                      ATTN_TQ, ATTN_TK)

    first_rows = jnp.pad(glu_meta, ((CONV_HALO - N_META, 0), (0, 0)))
    conv = _conv_branch(glu, first_rows, conv_w[0], conv_b[0], conv_ln_g[0], conv_ln_b[0], merge,
                        min(512, seq))

    h1 = _out_proj(attn, conv, w_out_b, x2, tm, 512)
    hn = _rmsnorm(h1, norm_mlp[0], BF16, rows=256)
    hid = _mlp_up(hn, w_up_b, tm, 1024)
    h2 = _mlp_down(hid, w_down_b, h1, tm, 1024, min(2048, d_ff))
    out = _rmsnorm(h2, norm_final, F32, rows=256)
    return out[None]
```

```python
import functools
import math

import jax
import jax.numpy as jnp
from jax import lax
from jax.experimental import pallas as pl
from jax.experimental.pallas import tpu as pltpu

F32 = jnp.float32
BF16 = jnp.bfloat16

N_META = 16
N_DIFF_HEADS = 8
N_SOFTMAX_MAPS = 2 * N_DIFF_HEADS
DK = 128
DV = 2 * DK
CONV_KERNEL = 31
REL_BUCKETS = 32
REL_MAX_EXACT = REL_BUCKETS // 2
REL_MAX_DISTANCE = 128
NORM_EPS = 1e-6
NEG_INF = -1e30
LAMBDA_INIT = 0.8 - 0.6 * math.exp(-0.3 * 0)
QK_SCALE = DK ** -0.5
EXP2_SCALE = QK_SCALE * math.log2(math.e)

LANES = 128
SUBLANES = 8
META_PAD = LANES
VMEM_LIMIT = 56 * 1024 * 1024

ATTN_TQ = 512
ATTN_TK = 512
ATTN_RC = 64
CONV_HALO = 32
CONV_RB = 64


def _params(*sem):
    return pltpu.CompilerParams(dimension_semantics=sem, vmem_limit_bytes=VMEM_LIMIT)


def _rmsnorm_kernel(x_ref, g_ref, o_ref):
    x = x_ref[...]
    y = x * lax.rsqrt(jnp.mean(x * x, axis=-1, keepdims=True) + NORM_EPS)
    o_ref[...] = (y * g_ref[...]).astype(o_ref.dtype)


def _rmsnorm(x, g, out_dtype, rows):
    m, d = x.shape
    return pl.pallas_call(
        _rmsnorm_kernel,
        grid=(m // rows,),
        in_specs=[pl.BlockSpec((rows, d), lambda i: (i, 0)),
                  pl.BlockSpec((1, d), lambda i: (0, 0))],
        out_specs=pl.BlockSpec((rows, d), lambda i: (i, 0)),
        out_shape=jax.ShapeDtypeStruct((m, d), out_dtype),
        compiler_params=_params("parallel"),
        name="rmsnorm",
    )(x, g.reshape(1, d))


def _mm_kernel(a_ref, w_ref, o_ref):
    o_ref[...] = jnp.dot(a_ref[...], w_ref[...], preferred_element_type=F32).astype(o_ref.dtype)


def _qkv_proj(a, w, n_cols, tm, tn):
    m, k = a.shape
    return pl.pallas_call(
        _mm_kernel,
        grid=(m // tm, n_cols // tn),
        in_specs=[pl.BlockSpec((tm, k), lambda i, j: (i, 0)),
                  pl.BlockSpec((k, tn), lambda i, j: (0, j))],
        out_specs=pl.BlockSpec((tm, tn), lambda i, j: (i, j)),
        out_shape=jax.ShapeDtypeStruct((m, n_cols), BF16),
        compiler_params=_params("parallel", "arbitrary"),
        name="qkv_proj",
    )(a, w)


def _glu_kernel(a_ref, wa_ref, wg_ref, o_ref):
    a = a_ref[...]
    val = jnp.dot(a, wa_ref[...], preferred_element_type=F32)
    gate = jnp.dot(a, wg_ref[...], preferred_element_type=F32)
    o_ref[...] = val * jax.nn.sigmoid(gate)


def _glu_proj(a, w, col0, width, tm, tn):
    m, k = a.shape
    off_a, off_g = col0 // tn, (col0 + width) // tn
    return pl.pallas_call(
        _glu_kernel,
        grid=(m // tm, width // tn),
        in_specs=[pl.BlockSpec((tm, k), lambda i, j: (i, 0)),
                  pl.BlockSpec((k, tn), lambda i, j: (0, off_a + j)),
                  pl.BlockSpec((k, tn), lambda i, j: (0, off_g + j))],
        out_specs=pl.BlockSpec((tm, tn), lambda i, j: (i, j)),
        out_shape=jax.ShapeDtypeStruct((m, width), F32),
        compiler_params=_params("parallel", "arbitrary"),
        name="glu_proj",
    )(a, w, w)


def _outproj_kernel(attn_ref, conv_ref, w1_ref, w2_ref, x_ref, o_ref):
    acc = jnp.dot(attn_ref[...], w1_ref[...], preferred_element_type=F32)
    acc = acc + jnp.dot(conv_ref[...], w2_ref[...], preferred_element_type=F32)
    o_ref[...] = x_ref[...] + acc


def _out_proj(attn, conv, w, x, tm, tn):
    m, ka = attn.shape
    kc = conv.shape[1]
    n = w.shape[1]
    assert ka == kc
    return pl.pallas_call(
        _outproj_kernel,
        grid=(m // tm, n // tn),
        in_specs=[pl.BlockSpec((tm, ka), lambda i, j: (i, 0)),
                  pl.BlockSpec((tm, kc), lambda i, j: (i, 0)),
                  pl.BlockSpec((ka, tn), lambda i, j: (0, j)),
                  pl.BlockSpec((kc, tn), lambda i, j: (1, j)),
                  pl.BlockSpec((tm, tn), lambda i, j: (i, j))],
        out_specs=pl.BlockSpec((tm, tn), lambda i, j: (i, j)),
        out_shape=jax.ShapeDtypeStruct((m, n), F32),
        compiler_params=_params("parallel", "arbitrary"),
        name="out_proj",
    )(attn, conv, w, w, x)


def _up_kernel(a_ref, w_ref, o_ref):
    h = jnp.maximum(jnp.dot(a_ref[...], w_ref[...], preferred_element_type=F32), 0.0)
    o_ref[...] = (h * h).astype(o_ref.dtype)


def _mlp_up(a, w, tm, tn):
    m, k = a.shape
    n = w.shape[1]
    return pl.pallas_call(
        _up_kernel,
        grid=(m // tm, n // tn),
        in_specs=[pl.BlockSpec((tm, k), lambda i, j: (i, 0)),
                  pl.BlockSpec((k, tn), lambda i, j: (0, j))],
        out_specs=pl.BlockSpec((tm, tn), lambda i, j: (i, j)),
        out_shape=jax.ShapeDtypeStruct((m, n), BF16),
        compiler_params=_params("parallel", "arbitrary"),
        name="mlp_up",
    )(a, w)


def _down_kernel(a_ref, w_ref, h_ref, o_ref):
    part = jnp.dot(a_ref[...], w_ref[...], preferred_element_type=F32)

    @pl.when(pl.program_id(2) == 0)
    def _():
        o_ref[...] = h_ref[...] + part

    @pl.when(pl.program_id(2) > 0)
    def _():
        o_ref[...] += part


def _mlp_down(a, w, h, tm, tn, tk):
    m, k = a.shape
    n = w.shape[1]
    return pl.pallas_call(
        _down_kernel,
        grid=(m // tm, n // tn, k // tk),
        in_specs=[pl.BlockSpec((tm, tk), lambda i, j, kk: (i, kk)),
                  pl.BlockSpec((tk, tn), lambda i, j, kk: (kk, j)),
                  pl.BlockSpec((tm, tn), lambda i, j, kk: (i, j))],
        out_specs=pl.BlockSpec((tm, tn), lambda i, j, kk: (i, j)),
        out_shape=jax.ShapeDtypeStruct((m, n), F32),
        compiler_params=_params("parallel", "parallel", "arbitrary"),
        name="mlp_down",
    )(a, w, h)


def _bias_kernel(rb_ref, near_ref, meta_ref, *, tq, tk):
    h = pl.program_id(0)
    far = rb_ref[REL_BUCKETS - 1, h]
    to_raw = 1.0 / QK_SCALE

    def table(dist):
        n = jnp.maximum(dist, 0)
        nf = jnp.maximum(n, 1).astype(F32)
        large = REL_MAX_EXACT + (jnp.log(nf / REL_MAX_EXACT)
                                 / math.log(REL_MAX_DISTANCE / REL_MAX_EXACT)
                                 * (REL_BUCKETS - REL_MAX_EXACT)).astype(jnp.int32)
        large = jnp.minimum(large, REL_BUCKETS - 1)
        bucket = jnp.where(n < REL_MAX_EXACT, n, large)
        out = jnp.zeros(dist.shape, F32)
        for b in range(REL_BUCKETS):
            out = jnp.where(bucket == b, rb_ref[b, h], out)
        return (out - far) * to_raw

    masked = NEG_INF * to_raw
    r = lax.broadcasted_iota(jnp.int32, (tq, tk), 0)
    c = lax.broadcasted_iota(jnp.int32, (tq, tk), 1)
    near_ref[0, 0] = jnp.where(r >= c, table(r - c), masked)
    near_ref[0, 1] = table(tk + r - c)
    rm = lax.broadcasted_iota(jnp.int32, (tq, META_PAD), 0)
    cm = lax.broadcasted_iota(jnp.int32, (tq, META_PAD), 1)
    valid = cm < N_META
    meta_ref[0, 0] = jnp.where(valid, table(N_META + rm - cm), masked)
    meta_ref[0, 1] = jnp.where(valid, 0.0, masked)


def _bias_tiles(rel_bias, tq, tk):
    assert tq == tk and tk + 1 >= REL_MAX_DISTANCE and tq + 1 >= REL_MAX_DISTANCE
    return pl.pallas_call(
        functools.partial(_bias_kernel, tq=tq, tk=tk),
        grid=(N_SOFTMAX_MAPS,),
        in_specs=[pl.BlockSpec(memory_space=pltpu.SMEM)],
        out_specs=[pl.BlockSpec((1, 2, tq, tk), lambda h: (h, 0, 0, 0)),
                   pl.BlockSpec((1, 2, tq, META_PAD), lambda h: (h, 0, 0, 0))],
        out_shape=[jax.ShapeDtypeStruct((N_SOFTMAX_MAPS, 2, tq, tk), F32),
                   jax.ShapeDtypeStruct((N_SOFTMAX_MAPS, 2, tq, META_PAD), F32)],
        compiler_params=_params("parallel"),
        name="bias_tiles",
    )(rel_bias)


def _attn_kernel(lq1_ref, lk1_ref, lq2_ref, lk2_ref, q_ref, k_ref, v_ref, km_ref, vm_ref,
                 near_ref, meta_ref, subln_ref, merge_ref, o_ref,
                 sa_sc, sb_sc, p_sc, m_sc, l_sc, a_sc, acc_sc, *, tq, tk):
    qi = pl.program_id(1)
    n_lane_tiles = tk // LANES

    def qk(j, s_ref):
        k0 = pl.multiple_of(j * tk, tk)
        for mp in range(2):
            q = q_ref[:, mp * DK:(mp + 1) * DK]
            k = k_ref[pl.ds(k0, tk), mp * DK:(mp + 1) * DK]
            s_ref[mp] = lax.dot_general(q, k, (((1,), (1,)), ((), ())), preferred_element_type=F32)

    def softmax_pv(j, s_ref, slot, bias_variant=None):
        v0 = pl.multiple_of(j * tk, tk)
        v = v_ref[pl.ds(v0, tk), :]
        for mp in range(2):
            for c in range(tq // ATTN_RC):
                rows = slice(c * ATTN_RC, (c + 1) * ATTN_RC)
                s = s_ref[mp, rows, :]
                if bias_variant is not None:
                    s = s + near_ref[mp, bias_variant, rows, :]
                m_prev = m_sc[mp, rows, :]
                m_new = jnp.maximum(m_prev, jnp.max(s, axis=1, keepdims=True))
                alpha = jnp.exp2((m_prev - m_new) * EXP2_SCALE)
                p = jnp.exp2((s - pltpu.repeat(m_new, n_lane_tiles, axis=1)) * EXP2_SCALE)
                part = p[:, 0:LANES]
                for t in range(1, n_lane_tiles):
                    part = part + p[:, t * LANES:(t + 1) * LANES]
                l_sc[mp, rows, :] = alpha * l_sc[mp, rows, :] + part
                m_sc[mp, rows, :] = m_new
                a_sc[mp, rows, :] = alpha
                p_sc[slot, mp, rows, :] = p.astype(BF16)
            pv = jnp.dot(p_sc[slot, mp], v, preferred_element_type=F32)
            acc_sc[mp] = pltpu.repeat(a_sc[mp], DV // LANES, axis=1) * acc_sc[mp] + pv

    var = jnp.minimum(qi, 1)
    for mp in range(2):
        q = q_ref[:, mp * DK:(mp + 1) * DK]
        k = km_ref[:, mp * DK:(mp + 1) * DK]
        s = lax.dot_general(q, k, (((1,), (1,)), ((), ())), preferred_element_type=F32)
        s = s + meta_ref[mp, var]
        m = jnp.max(s, axis=1, keepdims=True)
        p = jnp.exp2((s - m) * EXP2_SCALE)
        m_sc[mp] = jnp.broadcast_to(m, (tq, LANES))
        l_sc[mp] = p
        acc_sc[mp] = jnp.dot(p.astype(BF16), vm_ref[...], preferred_element_type=F32)

    n_far = jnp.maximum(qi - 1, 0)
    n_pairs = n_far // 2

    @pl.when(n_pairs > 0)
    def _():
        qk(0, sa_sc)

    def pair(i, carry):
        a = 2 * i
        qk(a + 1, sb_sc)
        softmax_pv(a, sa_sc, 0)
        qk(a + 2, sa_sc)
        softmax_pv(a + 1, sb_sc, 1)
        return carry

    lax.fori_loop(0, n_pairs, pair, 0)

    @pl.when(n_far % 2 == 1)
    def _():
        qk(n_far - 1, sa_sc)
        softmax_pv(n_far - 1, sa_sc, 0)

    @pl.when(qi >= 1)
    def _():
        qk(qi - 1, sb_sc)
        qk(qi, sa_sc)
        softmax_pv(qi - 1, sb_sc, 1, bias_variant=1)
        softmax_pv(qi, sa_sc, 0, bias_variant=0)

    @pl.when(qi == 0)
    def _():
        qk(qi, sa_sc)
        softmax_pv(qi, sa_sc, 0, bias_variant=0)

    lam = (jnp.exp(jnp.sum(lq1_ref[...] * lk1_ref[...], axis=-1, keepdims=True))
           - jnp.exp(jnp.sum(lq2_ref[...] * lk2_ref[...], axis=-1, keepdims=True))
           + LAMBDA_INIT)
    l0 = jnp.sum(l_sc[0], axis=1, keepdims=True)
    l1 = jnp.sum(l_sc[1], axis=1, keepdims=True)
    o = acc_sc[0] / l0 - lam * (acc_sc[1] / l1)
    y = o * lax.rsqrt(jnp.mean(o * o, axis=-1, keepdims=True) + NORM_EPS)
    y = (y * subln_ref[...]) * (1.0 - LAMBDA_INIT)
    o_ref[...] = (y * merge_ref[...]).astype(o_ref.dtype)


def _attention(qkv, qkv_meta, near, meta, lams, subln, merge, tq, tk):
    s = qkv.shape[0]
    assert tq == tk and s % tq == 0 and tq % ATTN_RC == 0
    hq, hk, hv = 0, N_DIFF_HEADS, 2 * N_DIFF_HEADS
    vec = lambda: pl.BlockSpec((1, DK), lambda h, i: (0, 0))
    return pl.pallas_call(
        functools.partial(_attn_kernel, tq=tq, tk=tk),
        grid=(N_DIFF_HEADS, s // tq),
        in_specs=[vec(), vec(), vec(), vec(),
                  pl.BlockSpec((tq, DV), lambda h, i: (i, hq + h)),
                  pl.BlockSpec((s, DV), lambda h, i: (0, hk + h)),
                  pl.BlockSpec((s, DV), lambda h, i: (0, hv + h)),
                  pl.BlockSpec((META_PAD, DV), lambda h, i: (0, hk + h)),
                  pl.BlockSpec((META_PAD, DV), lambda h, i: (0, hv + h)),
                  pl.BlockSpec((2, 2, tq, tk), lambda h, i: (h, 0, 0, 0)),
                  pl.BlockSpec((2, 2, tq, META_PAD), lambda h, i: (h, 0, 0, 0)),
                  pl.BlockSpec((1, DV), lambda h, i: (0, 0)),
                  pl.BlockSpec((1, DV), lambda h, i: (0, h))],
        out_specs=pl.BlockSpec((tq, DV), lambda h, i: (i, h)),
        out_shape=jax.ShapeDtypeStruct((s, N_DIFF_HEADS * DV), BF16),
        scratch_shapes=[pltpu.VMEM((2, tq, tk), F32),
                        pltpu.VMEM((2, tq, tk), F32),
                        pltpu.VMEM((2, 2, tq, tk), BF16),
                        pltpu.VMEM((2, tq, LANES), F32),
                        pltpu.VMEM((2, tq, LANES), F32),
                        pltpu.VMEM((2, tq, LANES), F32),
                        pltpu.VMEM((2, tq, DV), F32)],
        compiler_params=_params("parallel", "arbitrary"),
        name="diff_attention",
    )(*lams, qkv, qkv, qkv, qkv_meta, qkv_meta, near, meta, subln, merge)


def _conv_kernel(g_ref, prev_ref, first_ref, w_ref, b_ref, lng_ref, lnb_ref, merge_ref, o_ref,
                 ext_sc, y_sc, *, tm, c):
    i = pl.program_id(0)
    d0 = CONV_HALO - (CONV_KERNEL - 1)

    @pl.when(i == 0)
    def _():
        ext_sc[0:CONV_HALO, :] = first_ref[...]

    @pl.when(i > 0)
    def _():
        ext_sc[0:CONV_HALO, :] = prev_ref[...]

    ext_sc[CONV_HALO:CONV_HALO + tm, :] = g_ref[...]
    ext_sc[CONV_HALO + tm:CONV_HALO + tm + SUBLANES, :] = jnp.zeros((SUBLANES, c), F32)

    rb = CONV_RB
    n_load = rb + SUBLANES + CONV_HALO

    def lane_group(lg, carry):
        l0 = pl.multiple_of(lg * LANES, LANES)
        w = w_ref[:, pl.ds(l0, LANES)]
        bias = b_ref[:, pl.ds(l0, LANES)]

        def row_block(r, carry2):
            r0 = pl.multiple_of(r * rb, rb)
            e = ext_sc[pl.ds(r0, n_load), pl.ds(l0, LANES)]
            y = jnp.broadcast_to(bias, (rb, LANES))
            for s in range(SUBLANES):
                q = None
                for k in range((CONV_HALO + SUBLANES) // SUBLANES):
                    j = s + SUBLANES * k - d0
                    if 0 <= j < CONV_KERNEL:
                        term = w[j:j + 1, :] * e[SUBLANES * k:SUBLANES * k + rb + SUBLANES, :]
                        q = term if q is None else q + term
                y = y + q[s:s + rb, :]
            y_sc[pl.ds(r0, rb), pl.ds(l0, LANES)] = y
            return carry2

        return lax.fori_loop(0, tm // rb, row_block, carry)

    lax.fori_loop(0, c // LANES, lane_group, 0)

    def norm_block(r, carry):
        r0 = pl.multiple_of(r * SUBLANES, SUBLANES)
        y = y_sc[pl.ds(r0, SUBLANES), :]
        mu = jnp.mean(y, axis=-1, keepdims=True)
        var = jnp.mean(jnp.square(y - mu), axis=-1, keepdims=True)
        z = (y - mu) * lax.rsqrt(var + NORM_EPS) * lng_ref[...] + lnb_ref[...]
        z = z * jax.nn.sigmoid(z)
        o_ref[pl.ds(r0, SUBLANES), :] = (z * merge_ref[...]).astype(o_ref.dtype)
        return carry

    lax.fori_loop(0, tm // SUBLANES, norm_block, 0)


def _conv_branch(g, first_rows, conv_w, conv_b, ln_g, ln_b, merge, tm):
    s, c = g.shape
    assert tm % CONV_HALO == 0 and tm % CONV_RB == 0 and CONV_HALO >= CONV_KERNEL - 1
    halo_blocks = tm // CONV_HALO
    row = lambda: pl.BlockSpec((1, c), lambda i: (0, 0))
    return pl.pallas_call(
        functools.partial(_conv_kernel, tm=tm, c=c),
        grid=(s // tm,),
        in_specs=[pl.BlockSpec((tm, c), lambda i: (i, 0)),
                  pl.BlockSpec((CONV_HALO, c), lambda i: (jnp.maximum(i * halo_blocks - 1, 0), 0)),
                  pl.BlockSpec((CONV_HALO, c), lambda i: (0, 0)),
                  pl.BlockSpec((CONV_KERNEL, c), lambda i: (0, 0)),
                  row(), row(), row(),
                  pl.BlockSpec((1, c), lambda i: (0, 1))],
        out_specs=pl.BlockSpec((tm, c), lambda i: (i, 0)),
        out_shape=jax.ShapeDtypeStruct((s, c), BF16),
        scratch_shapes=[pltpu.VMEM((CONV_HALO + tm + SUBLANES, c), F32),
                        pltpu.VMEM((tm, c), F32)],
        compiler_params=_params("parallel"),
        name="conv_branch",
    )(g, g, first_rows, conv_w, conv_b.reshape(1, c), ln_g.reshape(1, c), ln_b.reshape(1, c), merge)


def kernel(x, meta_tokens, rel_bias, norm_mix, w_in, lambda_q1, lambda_k1, lambda_q2, lambda_k2,
           subln_w, conv_w, conv_b, conv_ln_g, conv_ln_b, merge_scale, w_out, norm_mlp, w_up,
           w_down, norm_final):
    batch, seq, d = x.shape
    assert batch == 1 and w_in.shape[0] == 1, "single sequence, single layer"
    attn_w = N_DIFF_HEADS * DV
    conv_c = conv_w.shape[-1]
    assert w_in.shape[-1] == 3 * attn_w + 2 * conv_c and meta_tokens.shape[0] == N_META
    d_ff = w_up.shape[-1]

    tm = min(1024, seq)
    x2 = x[0]
    w_in_b = w_in[0].astype(BF16)
    w_out_b = w_out[0].astype(BF16)
    w_up_b = w_up[0].astype(BF16)
    w_down_b = w_down[0].astype(BF16)
    merge = merge_scale[0].reshape(1, attn_w + conv_c)

    xn = _rmsnorm(x2, norm_mix[0], BF16, rows=256)
    xn_meta = _rmsnorm(meta_tokens, norm_mix[0], BF16, rows=N_META)
    qkv = _qkv_proj(xn, w_in_b, 3 * attn_w, tm, 1024)
    qkv_meta = _qkv_proj(xn_meta, w_in_b, 3 * attn_w, N_META, 1024)
    glu = _glu_proj(xn, w_in_b, 3 * attn_w, conv_c, tm, 512)
    glu_meta = _glu_proj(xn_meta, w_in_b, 3 * attn_w, conv_c, N_META, 512)

    near, meta_bias = _bias_tiles(rel_bias, ATTN_TQ, ATTN_TK)
    qkv_meta_pad = jnp.pad(qkv_meta, ((0, META_PAD - N_META), (0, 0)))
    lams = [v[0].reshape(1, DK) for v in (lambda_q1, lambda_k1, lambda_q2, lambda_k2)]
    attn = _attention(qkv, qkv_meta_pad, near, meta_bias, lams, subln_w[0].reshape(1, DV), merge,
                      ATTN_TQ, ATTN_TK)

    first_rows = jnp.pad(glu_meta, ((CONV_HALO - N_META, 0), (0, 0)))
    conv = _conv_branch(glu, first_rows, conv_w[0], conv_b[0], conv_ln_g[0], conv_ln_b[0], merge,
                        min(512, seq))

    h1 = _out_proj(attn, conv, w_out_b, x2, tm, 512)
    hn = _rmsnorm(h1, norm_mlp[0], BF16, rows=256)
    hid = _mlp_up(hn, w_up_b, tm, 1024)
    h2 = _mlp_down(hid, w_down_b, h1, tm, 1024, min(2048, d_ff))
    out = _rmsnorm(h2, norm_final, F32, rows=256)
    return out[None]
```

```python
import functools
import math

import jax
import jax.numpy as jnp
from jax import lax
from jax.experimental import pallas as pl
from jax.experimental.pallas import tpu as pltpu

F32 = jnp.float32
BF16 = jnp.bfloat16

N_META = 16
N_DIFF_HEADS = 8
N_SOFTMAX_MAPS = 2 * N_DIFF_HEADS
DK = 128
DV = 2 * DK
CONV_KERNEL = 31
REL_BUCKETS = 32
REL_MAX_EXACT = REL_BUCKETS // 2
REL_MAX_DISTANCE = 128
NORM_EPS = 1e-6
NEG_INF = -1e30
LAMBDA_INIT = 0.8 - 0.6 * math.exp(-0.3 * 0)
QK_SCALE = DK ** -0.5
EXP2_SCALE = QK_SCALE * math.log2(math.e)

LANES = 128
SUBLANES = 8
META_PAD = LANES
VMEM_LIMIT = 56 * 1024 * 1024

ATTN_TK = 512
ATTN_TQ = 2 * ATTN_TK
ATTN_RC = 64
CONV_HALO = 32
CONV_RB = 64
CONV_NB = 16


def _params(*sem):
    return pltpu.CompilerParams(dimension_semantics=sem, vmem_limit_bytes=VMEM_LIMIT)


def _rmsnorm_kernel(x_ref, g_ref, o_ref):
    x = x_ref[...]
    y = x * lax.rsqrt(jnp.mean(x * x, axis=-1, keepdims=True) + NORM_EPS)
    o_ref[...] = (y * g_ref[...]).astype(o_ref.dtype)


def _rmsnorm(x, g, out_dtype, rows):
    m, d = x.shape
    return pl.pallas_call(
        _rmsnorm_kernel,
        grid=(m // rows,),
        in_specs=[pl.BlockSpec((rows, d), lambda i: (i, 0)),
                  pl.BlockSpec((1, d), lambda i: (0, 0))],
        out_specs=pl.BlockSpec((rows, d), lambda i: (i, 0)),
        out_shape=jax.ShapeDtypeStruct((m, d), out_dtype),
        compiler_params=_params("parallel"),
        name="rmsnorm",
    )(x, g.reshape(1, d))


def _mm_kernel(a_ref, w_ref, o_ref):
    o_ref[...] = jnp.dot(a_ref[...], w_ref[...], preferred_element_type=F32).astype(o_ref.dtype)


def _qkv_proj(a, w, n_cols, tm, tn):
    m, k = a.shape
    return pl.pallas_call(
        _mm_kernel,
        grid=(m // tm, n_cols // tn),
        in_specs=[pl.BlockSpec((tm, k), lambda i, j: (i, 0)),
                  pl.BlockSpec((k, tn), lambda i, j: (0, j))],
        out_specs=pl.BlockSpec((tm, tn), lambda i, j: (i, j)),
        out_shape=jax.ShapeDtypeStruct((m, n_cols), BF16),
        compiler_params=_params("parallel", "arbitrary"),
        name="qkv_proj",
    )(a, w)


def _glu_kernel(a_ref, wa_ref, wg_ref, o_ref):
    a = a_ref[...]
    val = jnp.dot(a, wa_ref[...], preferred_element_type=F32)
    gate = jnp.dot(a, wg_ref[...], preferred_element_type=F32)
    o_ref[...] = val * jax.nn.sigmoid(gate)


def _glu_proj(a, w, col0, width, tm, tn):
    m, k = a.shape
    off_a, off_g = col0 // tn, (col0 + width) // tn
    return pl.pallas_call(
        _glu_kernel,
        grid=(m // tm, width // tn),
        in_specs=[pl.BlockSpec((tm, k), lambda i, j: (i, 0)),
                  pl.BlockSpec((k, tn), lambda i, j: (0, off_a + j)),
                  pl.BlockSpec((k, tn), lambda i, j: (0, off_g + j))],
        out_specs=pl.BlockSpec((tm, tn), lambda i, j: (i, j)),
        out_shape=jax.ShapeDtypeStruct((m, width), F32),
        compiler_params=_params("parallel", "arbitrary"),
        name="glu_proj",
    )(a, w, w)


def _outproj_kernel(attn_ref, conv_ref, w1_ref, w2_ref, x_ref, o_ref):
    acc = jnp.dot(attn_ref[...], w1_ref[...], preferred_element_type=F32)
    acc = acc + jnp.dot(conv_ref[...], w2_ref[...], preferred_element_type=F32)
    o_ref[...] = x_ref[...] + acc


def _out_proj(attn, conv, w, x, tm, tn):
    m, ka = attn.shape
    kc = conv.shape[1]
    n = w.shape[1]
    assert ka == kc
    return pl.pallas_call(
        _outproj_kernel,
        grid=(m // tm, n // tn),
        in_specs=[pl.BlockSpec((tm, ka), lambda i, j: (i, 0)),
                  pl.BlockSpec((tm, kc), lambda i, j: (i, 0)),
                  pl.BlockSpec((ka, tn), lambda i, j: (0, j)),
                  pl.BlockSpec((kc, tn), lambda i, j: (1, j)),
                  pl.BlockSpec((tm, tn), lambda i, j: (i, j))],
        out_specs=pl.BlockSpec((tm, tn), lambda i, j: (i, j)),
        out_shape=jax.ShapeDtypeStruct((m, n), F32),
        compiler_params=_params("parallel", "arbitrary"),
        name="out_proj",
    )(attn, conv, w, w, x)


def _up_kernel(a_ref, w_ref, o_ref):
    h = jnp.maximum(jnp.dot(a_ref[...], w_ref[...], preferred_element_type=F32), 0.0)
    o_ref[...] = (h * h).astype(o_ref.dtype)


def _mlp_up(a, w, tm, tn):
    m, k = a.shape
    n = w.shape[1]
    return pl.pallas_call(
        _up_kernel,
        grid=(m // tm, n // tn),
        in_specs=[pl.BlockSpec((tm, k), lambda i, j: (i, 0)),
                  pl.BlockSpec((k, tn), lambda i, j: (0, j))],
        out_specs=pl.BlockSpec((tm, tn), lambda i, j: (i, j)),
        out_shape=jax.ShapeDtypeStruct((m, n), BF16),
        compiler_params=_params("parallel", "arbitrary"),
        name="mlp_up",
    )(a, w)


def _down_kernel(a_ref, w_ref, h_ref, o_ref):
    @pl.when(pl.program_id(2) == 0)
    def _():
        o_ref[...] = h_ref[...]

    o_ref[...] += jnp.dot(a_ref[...], w_ref[...], preferred_element_type=F32)


def _mlp_down(a, w, h, tm, tn, tk):
    m, k = a.shape
    n = w.shape[1]
    return pl.pallas_call(
        _down_kernel,
        grid=(m // tm, n // tn, k // tk),
        in_specs=[pl.BlockSpec((tm, tk), lambda i, j, kk: (i, kk)),
                  pl.BlockSpec((tk, tn), lambda i, j, kk: (kk, j)),
                  pl.BlockSpec((tm, tn), lambda i, j, kk: (i, j))],
        out_specs=pl.BlockSpec((tm, tn), lambda i, j, kk: (i, j)),
        out_shape=jax.ShapeDtypeStruct((m, n), F32),
        compiler_params=_params("parallel", "parallel", "arbitrary"),
        name="mlp_down",
    )(a, w, h)


def _bias_kernel(rb_ref, near_ref, meta_ref, *, tq, tk):
    h = pl.program_id(0)
    far = rb_ref[REL_BUCKETS - 1, h]
    to_raw = 1.0 / QK_SCALE

    def table(dist):
        n = jnp.maximum(dist, 0)
        nf = jnp.maximum(n, 1).astype(F32)
        large = REL_MAX_EXACT + (jnp.log(nf / REL_MAX_EXACT)
                                 / math.log(REL_MAX_DISTANCE / REL_MAX_EXACT)
                                 * (REL_BUCKETS - REL_MAX_EXACT)).astype(jnp.int32)
        large = jnp.minimum(large, REL_BUCKETS - 1)
        bucket = jnp.where(n < REL_MAX_EXACT, n, large)
        out = jnp.zeros(dist.shape, F32)
        for b in range(REL_BUCKETS):
            out = jnp.where(bucket == b, rb_ref[b, h], out)
        return (out - far) * to_raw

    masked = NEG_INF * to_raw
    r = lax.broadcasted_iota(jnp.int32, (tq, tk), 0)
    c = lax.broadcasted_iota(jnp.int32, (tq, tk), 1)
    near_ref[0, 0] = jnp.where(r >= c, table(r - c), masked)
    near_ref[0, 1] = table(tk + r - c)
    rm = lax.broadcasted_iota(jnp.int32, (tq, META_PAD), 0)
    cm = lax.broadcasted_iota(jnp.int32, (tq, META_PAD), 1)
    valid = cm < N_META
    meta_ref[0, 0] = jnp.where(valid, table(N_META + rm - cm), masked)
    meta_ref[0, 1] = jnp.where(valid, 0.0, masked)


def _bias_tiles(rel_bias, tq, tk):
    assert tq == tk and tk + 1 >= REL_MAX_DISTANCE and tq + 1 >= REL_MAX_DISTANCE
    return pl.pallas_call(
        functools.partial(_bias_kernel, tq=tq, tk=tk),
        grid=(N_SOFTMAX_MAPS,),
        in_specs=[pl.BlockSpec(memory_space=pltpu.SMEM)],
        out_specs=[pl.BlockSpec((1, 2, tq, tk), lambda h: (h, 0, 0, 0)),
                   pl.BlockSpec((1, 2, tq, META_PAD), lambda h: (h, 0, 0, 0))],
        out_shape=[jax.ShapeDtypeStruct((N_SOFTMAX_MAPS, 2, tq, tk), F32),
                   jax.ShapeDtypeStruct((N_SOFTMAX_MAPS, 2, tq, META_PAD), F32)],
        compiler_params=_params("parallel"),
        name="bias_tiles",
    )(rel_bias)


def _attn_kernel(lq1_ref, lk1_ref, lq2_ref, lk2_ref, q_ref, k_ref, v_ref, km_ref, vm_ref,
                 near_ref, meta_ref, subln_ref, merge_ref, o_ref,
                 sa_sc, sb_sc, p_sc, m_sc, l_sc, a_sc, acc_sc, *, tq, tk):
    qi = pl.program_id(1)
    half = tq // 2
    lane_tile = lambda x, n: jnp.concatenate([x] * n, axis=1)

    def qk(j, s_ref, r0=0, nrows=tq):
        k0 = pl.multiple_of(j * tk, tk)
        for mp in range(2):
            q = q_ref[r0:r0 + nrows, mp * DK:(mp + 1) * DK]
            k = k_ref[pl.ds(k0, tk), mp * DK:(mp + 1) * DK]
            s_ref[mp, r0:r0 + nrows, :] = lax.dot_general(
                q, k, (((1,), (1,)), ((), ())), preferred_element_type=F32)

    def softmax_pv(j, s_ref, slot, r0=0, nrows=tq, bias_variant=None):
        v0 = pl.multiple_of(j * tk, tk)
        v = v_ref[pl.ds(v0, tk), :]
        for mp in range(2):
            for c in range(nrows // ATTN_RC):
                rows = slice(r0 + c * ATTN_RC, r0 + (c + 1) * ATTN_RC)
                s = s_ref[mp, rows, :]
                if bias_variant is not None:
                    s = s + near_ref[mp, bias_variant, c * ATTN_RC:(c + 1) * ATTN_RC, :]
                m_prev = m_sc[mp, rows, :]
                m_new = jnp.maximum(m_prev, jnp.max(s, axis=1, keepdims=True))
                alpha = jnp.exp2((m_prev - m_new) * EXP2_SCALE)
                p = jnp.exp2((s - lane_tile(m_new, tk // LANES)) * EXP2_SCALE)
                part = p[:, 0:LANES]
                for t in range(1, tk // LANES):
                    part = part + p[:, t * LANES:(t + 1) * LANES]
                l_sc[mp, rows, :] = alpha * l_sc[mp, rows, :] + part
                m_sc[mp, rows, :] = m_new
                a_sc[mp, rows, :] = alpha
                p_sc[slot, mp, rows, :] = p.astype(BF16)
            blk = slice(r0, r0 + nrows)
            pv = jnp.dot(p_sc[slot, mp, blk, :], v, preferred_element_type=F32)
            acc_sc[mp, blk, :] = lane_tile(a_sc[mp, blk, :], DV // LANES) * acc_sc[mp, blk, :] + pv

    for mp in range(2):
        for r0, var in ((0, jnp.minimum(qi, 1)), (half, 1)):
            blk = slice(r0, r0 + half)
            q = q_ref[blk, mp * DK:(mp + 1) * DK]
            k = km_ref[:, mp * DK:(mp + 1) * DK]
            s = lax.dot_general(q, k, (((1,), (1,)), ((), ())), preferred_element_type=F32)
            s = s + meta_ref[mp, var]
            m = jnp.max(s, axis=1, keepdims=True)
            p = jnp.exp2((s - m) * EXP2_SCALE)
            m_sc[mp, blk, :] = jnp.broadcast_to(m, (half, LANES))
            l_sc[mp, blk, :] = p
            acc_sc[mp, blk, :] = jnp.dot(p.astype(BF16), vm_ref[...], preferred_element_type=F32)

    @pl.when(qi >= 1)
    def _():
        qk(0, sa_sc)

    def pair(i, carry):
        a = 2 * i
        qk(a + 1, sb_sc)
        softmax_pv(a, sa_sc, 0)
        qk(a + 2, sa_sc)
        softmax_pv(a + 1, sb_sc, 1)
        return carry

    lax.fori_loop(0, jnp.maximum(qi - 1, 0), pair, 0)

    @pl.when(qi >= 1)
    def _():
        jd = 2 * qi
        qk(jd - 1, sb_sc)
        softmax_pv(jd - 2, sa_sc, 0)
        softmax_pv(jd - 1, sb_sc, 1, 0, half, bias_variant=1)
        softmax_pv(jd - 1, sb_sc, 1, half, half)
        qk(jd, sa_sc)
        qk(jd + 1, sb_sc, half, half)
        softmax_pv(jd, sa_sc, 0, 0, half, bias_variant=0)
        softmax_pv(jd, sa_sc, 0, half, half, bias_variant=1)
        softmax_pv(jd + 1, sb_sc, 1, half, half, bias_variant=0)

    @pl.when(qi == 0)
    def _():
        qk(0, sa_sc)
        qk(1, sb_sc, half, half)
        softmax_pv(0, sa_sc, 0, 0, half, bias_variant=0)
        softmax_pv(0, sa_sc, 0, half, half, bias_variant=1)
        softmax_pv(1, sb_sc, 1, half, half, bias_variant=0)

    lam = (jnp.exp(jnp.sum(lq1_ref[...] * lk1_ref[...], axis=-1, keepdims=True))
           - jnp.exp(jnp.sum(lq2_ref[...] * lk2_ref[...], axis=-1, keepdims=True))
           + LAMBDA_INIT)
    l0 = jnp.sum(l_sc[0], axis=1, keepdims=True)
    l1 = jnp.sum(l_sc[1], axis=1, keepdims=True)
    o = acc_sc[0] / l0 - lam * (acc_sc[1] / l1)
    y = o * lax.rsqrt(jnp.mean(o * o, axis=-1, keepdims=True) + NORM_EPS)
    y = (y * subln_ref[...]) * (1.0 - LAMBDA_INIT)
    o_ref[...] = (y * merge_ref[...]).astype(o_ref.dtype)


def _attention(qkv, qkv_meta, near, meta, lams, subln, merge, tq, tk):
    s = qkv.shape[0]
    assert tq == 2 * tk and s % tq == 0 and tk % ATTN_RC == 0
    hq, hk, hv = 0, N_DIFF_HEADS, 2 * N_DIFF_HEADS
    vec = lambda: pl.BlockSpec((1, DK), lambda h, i: (0, 0))
    once = pl.Buffered(1)
    return pl.pallas_call(
        functools.partial(_attn_kernel, tq=tq, tk=tk),
        grid=(N_DIFF_HEADS, s // tq),
        in_specs=[vec(), vec(), vec(), vec(),
                  pl.BlockSpec((tq, DV), lambda h, i: (i, hq + h)),
                  pl.BlockSpec((s, DV), lambda h, i: (0, hk + h), pipeline_mode=once),
                  pl.BlockSpec((s, DV), lambda h, i: (0, hv + h), pipeline_mode=once),
                  pl.BlockSpec((META_PAD, DV), lambda h, i: (0, hk + h)),
                  pl.BlockSpec((META_PAD, DV), lambda h, i: (0, hv + h)),
                  pl.BlockSpec((2, 2, tk, tk), lambda h, i: (h, 0, 0, 0)),
                  pl.BlockSpec((2, 2, tk, META_PAD), lambda h, i: (h, 0, 0, 0)),
                  pl.BlockSpec((1, DV), lambda h, i: (0, 0)),
                  pl.BlockSpec((1, DV), lambda h, i: (0, h))],
        out_specs=pl.BlockSpec((tq, DV), lambda h, i: (i, h)),
        out_shape=jax.ShapeDtypeStruct((s, N_DIFF_HEADS * DV), BF16),
        scratch_shapes=[pltpu.VMEM((2, tq, tk), F32),
                        pltpu.VMEM((2, tq, tk), F32),
                        pltpu.VMEM((2, 2, tq, tk), BF16),
                        pltpu.VMEM((2, tq, LANES), F32),
                        pltpu.VMEM((2, tq, LANES), F32),
                        pltpu.VMEM((2, tq, LANES), F32),
                        pltpu.VMEM((2, tq, DV), F32)],
        compiler_params=_params("parallel", "arbitrary"),
        name="diff_attention",
    )(*lams, qkv, qkv, qkv, qkv_meta, qkv_meta, near, meta, subln, merge)


def _conv_kernel(g_ref, prev_ref, first_ref, w_ref, b_ref, lng_ref, lnb_ref, merge_ref, o_ref,
                 ext_sc, y_sc, *, tm, c):
    i = pl.program_id(0)
    d0 = CONV_HALO - (CONV_KERNEL - 1)

    @pl.when(i == 0)
    def _():
        ext_sc[0:CONV_HALO, :] = first_ref[...]

    @pl.when(i > 0)
    def _():
        ext_sc[0:CONV_HALO, :] = prev_ref[...]

    ext_sc[CONV_HALO:CONV_HALO + tm, :] = g_ref[...]
    ext_sc[CONV_HALO + tm:CONV_HALO + tm + SUBLANES, :] = jnp.zeros((SUBLANES, c), F32)

    rb = CONV_RB
    n_load = rb + SUBLANES + CONV_HALO

    def lane_group(lg, carry):
        l0 = pl.multiple_of(lg * LANES, LANES)
        w = w_ref[:, pl.ds(l0, LANES)]
        bias = b_ref[:, pl.ds(l0, LANES)]

        def row_block(r, carry2):
            r0 = pl.multiple_of(r * rb, rb)
            e = ext_sc[pl.ds(r0, n_load), pl.ds(l0, LANES)]
            y = jnp.broadcast_to(bias, (rb, LANES))
            for s in range(SUBLANES):
                q = None
                for k in range((CONV_HALO + SUBLANES) // SUBLANES):
                    j = s + SUBLANES * k - d0
                    if 0 <= j < CONV_KERNEL:
                        term = w[j:j + 1, :] * e[SUBLANES * k:SUBLANES * k + rb + SUBLANES, :]
                        q = term if q is None else q + term
                y = y + q[s:s + rb, :]
            y_sc[pl.ds(r0, rb), pl.ds(l0, LANES)] = y
            return carry2

        return lax.fori_loop(0, tm // rb, row_block, carry)

    lax.fori_loop(0, c // LANES, lane_group, 0)

    def norm_block(r, carry):
        r0 = pl.multiple_of(r * CONV_NB, CONV_NB)
        y = y_sc[pl.ds(r0, CONV_NB), :]
        mu = jnp.mean(y, axis=-1, keepdims=True)
        var = jnp.mean(jnp.square(y - mu), axis=-1, keepdims=True)
        z = (y - mu) * lax.rsqrt(var + NORM_EPS) * lng_ref[...] + lnb_ref[...]
        z = z * jax.nn.sigmoid(z)
        o_ref[pl.ds(r0, CONV_NB), :] = (z * merge_ref[...]).astype(o_ref.dtype)
        return carry

    lax.fori_loop(0, tm // CONV_NB, norm_block, 0, unroll=4)


def _conv_branch(g, first_rows, conv_w, conv_b, ln_g, ln_b, merge, tm):
    s, c = g.shape
    assert tm % CONV_HALO == 0 and tm % CONV_RB == 0 and CONV_HALO >= CONV_KERNEL - 1
    halo_blocks = tm // CONV_HALO
    row = lambda: pl.BlockSpec((1, c), lambda i: (0, 0))
    return pl.pallas_call(
        functools.partial(_conv_kernel, tm=tm, c=c),
        grid=(s // tm,),
        in_specs=[pl.BlockSpec((tm, c), lambda i: (i, 0)),
                  pl.BlockSpec((CONV_HALO, c), lambda i: (jnp.maximum(i * halo_blocks - 1, 0), 0)),
                  pl.BlockSpec((CONV_HALO, c), lambda i: (0, 0)),
                  pl.BlockSpec((CONV_KERNEL, c), lambda i: (0, 0)),
                  row(), row(), row(),
                  pl.BlockSpec((1, c), lambda i: (0, 1))],
        out_specs=pl.BlockSpec((tm, c), lambda i: (i, 0)),
        out_shape=jax.ShapeDtypeStruct((s, c), BF16),
        scratch_shapes=[pltpu.VMEM((CONV_HALO + tm + SUBLANES, c), F32),
                        pltpu.VMEM((tm, c), F32)],
        compiler_params=_params("parallel"),
        name="conv_branch",
    )(g, g, first_rows, conv_w, conv_b.reshape(1, c), ln_g.reshape(1, c), ln_b.reshape(1, c), merge)


def kernel(x, meta_tokens, rel_bias, norm_mix, w_in, lambda_q1, lambda_k1, lambda_q2, lambda_k2,
           subln_w, conv_w, conv_b, conv_ln_g, conv_ln_b, merge_scale, w_out, norm_mlp, w_up,
           w_down, norm_final):
    batch, seq, d = x.shape
    assert batch == 1 and w_in.shape[0] == 1, "single sequence, single layer"
    attn_w = N_DIFF_HEADS * DV
    conv_c = conv_w.shape[-1]
    assert w_in.shape[-1] == 3 * attn_w + 2 * conv_c and meta_tokens.shape[0] == N_META
    d_ff = w_up.shape[-1]

    tm = min(1024, seq)
    x2 = x[0]
    w_in_b = w_in[0].astype(BF16)
    w_out_b = w_out[0].astype(BF16)
    w_up_b = w_up[0].astype(BF16)
    w_down_b = w_down[0].astype(BF16)
    merge = merge_scale[0].reshape(1, attn_w + conv_c)

    xn = _rmsnorm(x2, norm_mix[0], BF16, rows=256)
    xn_meta = _rmsnorm(meta_tokens, norm_mix[0], BF16, rows=N_META)
    qkv = _qkv_proj(xn, w_in_b, 3 * attn_w, tm, 1024)
    qkv_meta = _qkv_proj(xn_meta, w_in_b, 3 * attn_w, N_META, 1024)
    glu = _glu_proj(xn, w_in_b, 3 * attn_w, conv_c, tm, 512)
    glu_meta = _glu_proj(xn_meta, w_in_b, 3 * attn_w, conv_c, N_META, 512)

    near, meta_bias = _bias_tiles(rel_bias, ATTN_TK, ATTN_TK)
    qkv_meta_pad = jnp.pad(qkv_meta, ((0, META_PAD - N_META), (0, 0)))
    lams = [v[0].reshape(1, DK) for v in (lambda_q1, lambda_k1, lambda_q2, lambda_k2)]
    attn = _attention(qkv, qkv_meta_pad, near, meta_bias, lams, subln_w[0].reshape(1, DV), merge,
                      ATTN_TQ, ATTN_TK)

    first_rows = jnp.pad(glu_meta, ((CONV_HALO - N_META, 0), (0, 0)))
    conv = _conv_branch(glu, first_rows, conv_w[0], conv_b[0], conv_ln_g[0], conv_ln_b[0], merge,
                        min(512, seq))

    h1 = _out_proj(attn, conv, w_out_b, x2, tm, 512)
    hn = _rmsnorm(h1, norm_mlp[0], BF16, rows=256)
    hid = _mlp_up(hn, w_up_b, tm, 1024)
    h2 = _mlp_down(hid, w_down_b, h1, tm, 1024, min(4096, d_ff))
    out = _rmsnorm(h2, norm_final, F32, rows=256)
    return out[None]
```

```python
import functools
import math

import jax
import jax.numpy as jnp
from jax import lax
from jax.experimental import pallas as pl
from jax.experimental.pallas import tpu as pltpu

F32 = jnp.float32
BF16 = jnp.bfloat16

N_META = 16
N_DIFF_HEADS = 8
N_SOFTMAX_MAPS = 2 * N_DIFF_HEADS
DK = 128
DV = 2 * DK
CONV_KERNEL = 31
REL_BUCKETS = 32
REL_MAX_EXACT = REL_BUCKETS // 2
REL_MAX_DISTANCE = 128
NORM_EPS = 1e-6
NEG_INF = -1e30
LAMBDA_INIT = 0.8 - 0.6 * math.exp(-0.3 * 0)
QK_SCALE = DK ** -0.5
EXP2_SCALE = QK_SCALE * math.log2(math.e)

LANES = 128
SUBLANES = 8
META_PAD = LANES
VMEM_LIMIT = 56 * 1024 * 1024

ATTN_TK = 512
ATTN_TQ = 2 * ATTN_TK
ATTN_RC = 64
CONV_HALO = 32
CONV_RB = 64
CONV_NB = 16


def _params(*sem):
    return pltpu.CompilerParams(dimension_semantics=sem, vmem_limit_bytes=VMEM_LIMIT)


def _rmsnorm_kernel(x_ref, g_ref, o_ref):
    x = x_ref[...]
    y = x * lax.rsqrt(jnp.mean(x * x, axis=-1, keepdims=True) + NORM_EPS)
    o_ref[...] = (y * g_ref[...]).astype(o_ref.dtype)


def _rmsnorm(x, g, out_dtype, rows):
    m, d = x.shape
    return pl.pallas_call(
        _rmsnorm_kernel,
        grid=(m // rows,),
        in_specs=[pl.BlockSpec((rows, d), lambda i: (i, 0)),
                  pl.BlockSpec((1, d), lambda i: (0, 0))],
        out_specs=pl.BlockSpec((rows, d), lambda i: (i, 0)),
        out_shape=jax.ShapeDtypeStruct((m, d), out_dtype),
        compiler_params=_params("parallel"),
        name="rmsnorm",
    )(x, g.reshape(1, d))


def _mm_kernel(a_ref, w_ref, cs_ref, o_ref):
    acc = jnp.dot(a_ref[...], w_ref[...], preferred_element_type=F32)
    o_ref[...] = (acc * cs_ref[...]).astype(o_ref.dtype)


def _qkv_proj(a, w, col_scale, n_cols, tm, tn):
    m, k = a.shape
    return pl.pallas_call(
        _mm_kernel,
        grid=(m // tm, n_cols // tn),
        in_specs=[pl.BlockSpec((tm, k), lambda i, j: (i, 0)),
                  pl.BlockSpec((k, tn), lambda i, j: (0, j)),
                  pl.BlockSpec((1, tn), lambda i, j: (0, j))],
        out_specs=pl.BlockSpec((tm, tn), lambda i, j: (i, j)),
        out_shape=jax.ShapeDtypeStruct((m, n_cols), BF16),
        compiler_params=_params("parallel", "arbitrary"),
        name="qkv_proj",
    )(a, w, col_scale)


def _glu_kernel(a_ref, wa_ref, wg_ref, o_ref):
    a = a_ref[...]
    val = jnp.dot(a, wa_ref[...], preferred_element_type=F32)
    gate = jnp.dot(a, wg_ref[...], preferred_element_type=F32)
    o_ref[...] = val * jax.nn.sigmoid(gate)


def _glu_proj(a, w, col0, width, tm, tn):
    m, k = a.shape
    off_a, off_g = col0 // tn, (col0 + width) // tn
    return pl.pallas_call(
        _glu_kernel,
        grid=(m // tm, width // tn),
        in_specs=[pl.BlockSpec((tm, k), lambda i, j: (i, 0)),
                  pl.BlockSpec((k, tn), lambda i, j: (0, off_a + j)),
                  pl.BlockSpec((k, tn), lambda i, j: (0, off_g + j))],
        out_specs=pl.BlockSpec((tm, tn), lambda i, j: (i, j)),
        out_shape=jax.ShapeDtypeStruct((m, width), F32),
        compiler_params=_params("parallel", "arbitrary"),
        name="glu_proj",
    )(a, w, w)


def _outproj_kernel(attn_ref, conv_ref, w1_ref, w2_ref, x_ref, o_ref):
    acc = jnp.dot(attn_ref[...], w1_ref[...], preferred_element_type=F32)
    acc = acc + jnp.dot(conv_ref[...], w2_ref[...], preferred_element_type=F32)
    o_ref[...] = x_ref[...] + acc


def _out_proj(attn, conv, w, x, tm, tn):
    m, ka = attn.shape
    kc = conv.shape[1]
    n = w.shape[1]
    assert ka == kc
    return pl.pallas_call(
        _outproj_kernel,
        grid=(m // tm, n // tn),
        in_specs=[pl.BlockSpec((tm, ka), lambda i, j: (i, 0)),
                  pl.BlockSpec((tm, kc), lambda i, j: (i, 0)),
                  pl.BlockSpec((ka, tn), lambda i, j: (0, j)),
                  pl.BlockSpec((kc, tn), lambda i, j: (1, j)),
                  pl.BlockSpec((tm, tn), lambda i, j: (i, j))],
        out_specs=pl.BlockSpec((tm, tn), lambda i, j: (i, j)),
        out_shape=jax.ShapeDtypeStruct((m, n), F32),
        compiler_params=_params("parallel", "arbitrary"),
        name="out_proj",
    )(attn, conv, w, w, x)


def _up_kernel(a_ref, w_ref, o_ref):
    h = jnp.maximum(jnp.dot(a_ref[...], w_ref[...], preferred_element_type=F32), 0.0)
    o_ref[...] = (h * h).astype(o_ref.dtype)


def _mlp_up(a, w, tm, tn):
    m, k = a.shape
    n = w.shape[1]
    return pl.pallas_call(
        _up_kernel,
        grid=(m // tm, n // tn),
        in_specs=[pl.BlockSpec((tm, k), lambda i, j: (i, 0)),
                  pl.BlockSpec((k, tn), lambda i, j: (0, j))],
        out_specs=pl.BlockSpec((tm, tn), lambda i, j: (i, j)),
        out_shape=jax.ShapeDtypeStruct((m, n), BF16),
        compiler_params=_params("parallel", "arbitrary"),
        name="mlp_up",
    )(a, w)


def _down_kernel(a_ref, w_ref, h_ref, o_ref):
    @pl.when(pl.program_id(2) == 0)
    def _():
        o_ref[...] = h_ref[...]

    o_ref[...] += jnp.dot(a_ref[...], w_ref[...], preferred_element_type=F32)


def _mlp_down(a, w, h, tm, tn, tk):
    m, k = a.shape
    n = w.shape[1]
    return pl.pallas_call(
        _down_kernel,
        grid=(m // tm, n // tn, k // tk),
        in_specs=[pl.BlockSpec((tm, tk), lambda i, j, kk: (i, kk)),
                  pl.BlockSpec((tk, tn), lambda i, j, kk: (kk, j)),
                  pl.BlockSpec((tm, tn), lambda i, j, kk: (i, j))],
        out_specs=pl.BlockSpec((tm, tn), lambda i, j, kk: (i, j)),
        out_shape=jax.ShapeDtypeStruct((m, n), F32),
        compiler_params=_params("parallel", "parallel", "arbitrary"),
        name="mlp_down",
    )(a, w, h)


def _bias_kernel(rb_ref, near_ref, meta_ref, *, tq, tk):
    h = pl.program_id(0)
    far = rb_ref[REL_BUCKETS - 1, h]
    to_log2 = math.log2(math.e)

    def table(dist):
        n = jnp.maximum(dist, 0)
        nf = jnp.maximum(n, 1).astype(F32)
        large = REL_MAX_EXACT + (jnp.log(nf / REL_MAX_EXACT)
                                 / math.log(REL_MAX_DISTANCE / REL_MAX_EXACT)
                                 * (REL_BUCKETS - REL_MAX_EXACT)).astype(jnp.int32)
        large = jnp.minimum(large, REL_BUCKETS - 1)
        bucket = jnp.where(n < REL_MAX_EXACT, n, large)
        out = jnp.zeros(dist.shape, F32)
        for b in range(REL_BUCKETS):
            out = jnp.where(bucket == b, rb_ref[b, h], out)
        return (out - far) * to_log2

    masked = NEG_INF * to_log2
    r = lax.broadcasted_iota(jnp.int32, (tq, tk), 0)
    c = lax.broadcasted_iota(jnp.int32, (tq, tk), 1)
    near_ref[0, 0] = jnp.where(r >= c, table(r - c), masked)
    near_ref[0, 1] = table(tk + r - c)
    rm = lax.broadcasted_iota(jnp.int32, (tq, META_PAD), 0)
    cm = lax.broadcasted_iota(jnp.int32, (tq, META_PAD), 1)
    valid = cm < N_META
    meta_ref[0, 0] = jnp.where(valid, table(N_META + rm - cm), masked)
    meta_ref[0, 1] = jnp.where(valid, 0.0, masked)


def _bias_tiles(rel_bias, tq, tk):
    assert tq == tk and tk + 1 >= REL_MAX_DISTANCE and tq + 1 >= REL_MAX_DISTANCE
    return pl.pallas_call(
        functools.partial(_bias_kernel, tq=tq, tk=tk),
        grid=(N_SOFTMAX_MAPS,),
        in_specs=[pl.BlockSpec(memory_space=pltpu.SMEM)],
        out_specs=[pl.BlockSpec((1, 2, tq, tk), lambda h: (h, 0, 0, 0)),
                   pl.BlockSpec((1, 2, tq, META_PAD), lambda h: (h, 0, 0, 0))],
        out_shape=[jax.ShapeDtypeStruct((N_SOFTMAX_MAPS, 2, tq, tk), F32),
                   jax.ShapeDtypeStruct((N_SOFTMAX_MAPS, 2, tq, META_PAD), F32)],
        compiler_params=_params("parallel"),
        name="bias_tiles",
    )(rel_bias)


def _attn_kernel(lq1_ref, lk1_ref, lq2_ref, lk2_ref, q_ref, k_ref, v_ref, km_ref, vm_ref,
                 near_ref, meta_ref, subln_ref, merge_ref, o_ref,
                 sa_sc, sb_sc, p_sc, m_sc, l_sc, a_sc, acc_sc, *, tq, tk):
    qi = pl.program_id(1)
    half = tq // 2
    lane_tile = lambda x, n: jnp.concatenate([x] * n, axis=1)

    def qk(j, s_ref, r0=0, nrows=tq):
        k0 = pl.multiple_of(j * tk, tk)
        for mp in range(2):
            q = q_ref[r0:r0 + nrows, mp * DK:(mp + 1) * DK]
            k = k_ref[pl.ds(k0, tk), mp * DK:(mp + 1) * DK]
            s_ref[mp, r0:r0 + nrows, :] = lax.dot_general(
                q, k, (((1,), (1,)), ((), ())), preferred_element_type=F32)

    def softmax_pv(j, s_ref, slot, r0=0, nrows=tq, bias_variant=None):
        v0 = pl.multiple_of(j * tk, tk)
        v = v_ref[pl.ds(v0, tk), :]
        for mp in range(2):
            for c in range(nrows // ATTN_RC):
                rows = slice(r0 + c * ATTN_RC, r0 + (c + 1) * ATTN_RC)
                s = s_ref[mp, rows, :]
                if bias_variant is not None:
                    s = s + near_ref[mp, bias_variant, c * ATTN_RC:(c + 1) * ATTN_RC, :]
                m_prev = m_sc[mp, rows, :]
                m_new = jnp.maximum(m_prev, jnp.max(s, axis=1, keepdims=True))
                alpha = jnp.exp2(m_prev - m_new)
                p = jnp.exp2(s - lane_tile(m_new, tk // LANES))
                part = p[:, 0:LANES]
                for t in range(1, tk // LANES):
                    part = part + p[:, t * LANES:(t + 1) * LANES]
                l_sc[mp, rows, :] = alpha * l_sc[mp, rows, :] + part
                m_sc[mp, rows, :] = m_new
                a_sc[mp, rows, :] = alpha
                p_sc[slot, mp, rows, :] = p.astype(BF16)
            blk = slice(r0, r0 + nrows)
            pv = jnp.dot(p_sc[slot, mp, blk, :], v, preferred_element_type=F32)
            acc_sc[mp, blk, :] = lane_tile(a_sc[mp, blk, :], DV // LANES) * acc_sc[mp, blk, :] + pv

    for mp in range(2):
        for r0, var in ((0, jnp.minimum(qi, 1)), (half, 1)):
            blk = slice(r0, r0 + half)
            q = q_ref[blk, mp * DK:(mp + 1) * DK]
            k = km_ref[:, mp * DK:(mp + 1) * DK]
            s = lax.dot_general(q, k, (((1,), (1,)), ((), ())), preferred_element_type=F32)
            s = s + meta_ref[mp, var]
            m = jnp.max(s, axis=1, keepdims=True)
            p = jnp.exp2(s - m)
            m_sc[mp, blk, :] = jnp.broadcast_to(m, (half, LANES))
            l_sc[mp, blk, :] = p
            acc_sc[mp, blk, :] = jnp.dot(p.astype(BF16), vm_ref[...], preferred_element_type=F32)

    @pl.when(qi >= 1)
    def _():
        qk(0, sa_sc)

    def pair(i, carry):
        a = 2 * i
        qk(a + 1, sb_sc)
        softmax_pv(a, sa_sc, 0)
        qk(a + 2, sa_sc)
        softmax_pv(a + 1, sb_sc, 1)
        return carry

    lax.fori_loop(0, jnp.maximum(qi - 1, 0), pair, 0)

    @pl.when(qi >= 1)
    def _():
        jd = 2 * qi
        qk(jd - 1, sb_sc)
        softmax_pv(jd - 2, sa_sc, 0)
        softmax_pv(jd - 1, sb_sc, 1, 0, half, bias_variant=1)
        softmax_pv(jd - 1, sb_sc, 1, half, half)
        qk(jd, sa_sc)
        qk(jd + 1, sb_sc, half, half)
        softmax_pv(jd, sa_sc, 0, 0, half, bias_variant=0)
        softmax_pv(jd, sa_sc, 0, half, half, bias_variant=1)
        softmax_pv(jd + 1, sb_sc, 1, half, half, bias_variant=0)

    @pl.when(qi == 0)
    def _():
        qk(0, sa_sc)
        qk(1, sb_sc, half, half)
        softmax_pv(0, sa_sc, 0, 0, half, bias_variant=0)
        softmax_pv(0, sa_sc, 0, half, half, bias_variant=1)
        softmax_pv(1, sb_sc, 1, half, half, bias_variant=0)

    lam = (jnp.exp(jnp.sum(lq1_ref[...] * lk1_ref[...], axis=-1, keepdims=True))
           - jnp.exp(jnp.sum(lq2_ref[...] * lk2_ref[...], axis=-1, keepdims=True))
           + LAMBDA_INIT)
    l0 = jnp.sum(l_sc[0], axis=1, keepdims=True)
    l1 = jnp.sum(l_sc[1], axis=1, keepdims=True)
    o = acc_sc[0] / l0 - lam * (acc_sc[1] / l1)
    y = o * lax.rsqrt(jnp.mean(o * o, axis=-1, keepdims=True) + NORM_EPS)
    y = (y * subln_ref[...]) * (1.0 - LAMBDA_INIT)
    o_ref[...] = (y * merge_ref[...]).astype(o_ref.dtype)


def _attention(qkv, qkv_meta, near, meta, lams, subln, merge, tq, tk):
    s = qkv.shape[0]
    assert tq == 2 * tk and s % tq == 0 and tk % ATTN_RC == 0
    hq, hk, hv = 0, N_DIFF_HEADS, 2 * N_DIFF_HEADS
    vec = lambda: pl.BlockSpec((1, DK), lambda h, i: (0, 0))
    once = pl.Buffered(1)
    return pl.pallas_call(
        functools.partial(_attn_kernel, tq=tq, tk=tk),
        grid=(N_DIFF_HEADS, s // tq),
        in_specs=[vec(), vec(), vec(), vec(),
                  pl.BlockSpec((tq, DV), lambda h, i: (i, hq + h)),
                  pl.BlockSpec((s, DV), lambda h, i: (0, hk + h), pipeline_mode=once),
                  pl.BlockSpec((s, DV), lambda h, i: (0, hv + h), pipeline_mode=once),
                  pl.BlockSpec((META_PAD, DV), lambda h, i: (0, hk + h)),
                  pl.BlockSpec((META_PAD, DV), lambda h, i: (0, hv + h)),
                  pl.BlockSpec((2, 2, tk, tk), lambda h, i: (h, 0, 0, 0)),
                  pl.BlockSpec((2, 2, tk, META_PAD), lambda h, i: (h, 0, 0, 0)),
                  pl.BlockSpec((1, DV), lambda h, i: (0, 0)),
                  pl.BlockSpec((1, DV), lambda h, i: (0, h))],
        out_specs=pl.BlockSpec((tq, DV), lambda h, i: (i, h)),
        out_shape=jax.ShapeDtypeStruct((s, N_DIFF_HEADS * DV), BF16),
        scratch_shapes=[pltpu.VMEM((2, tq, tk), F32),
                        pltpu.VMEM((2, tq, tk), F32),
                        pltpu.VMEM((2, 2, tq, tk), BF16),
                        pltpu.VMEM((2, tq, LANES), F32),
                        pltpu.VMEM((2, tq, LANES), F32),
                        pltpu.VMEM((2, tq, LANES), F32),
                        pltpu.VMEM((2, tq, DV), F32)],
        compiler_params=_params("parallel", "arbitrary"),
        name="diff_attention",
    )(*lams, qkv, qkv, qkv, qkv_meta, qkv_meta, near, meta, subln, merge)


def _conv_kernel(g_ref, prev_ref, first_ref, w_ref, b_ref, lng_ref, lnb_ref, merge_ref, o_ref,
                 ext_sc, y_sc, *, tm, c):
    i = pl.program_id(0)
    d0 = CONV_HALO - (CONV_KERNEL - 1)

    @pl.when(i == 0)
    def _():
        ext_sc[0:CONV_HALO, :] = first_ref[...]

    @pl.when(i > 0)
    def _():
        ext_sc[0:CONV_HALO, :] = prev_ref[...]

    ext_sc[CONV_HALO:CONV_HALO + tm, :] = g_ref[...]
    ext_sc[CONV_HALO + tm:CONV_HALO + tm + SUBLANES, :] = jnp.zeros((SUBLANES, c), F32)

    rb = CONV_RB
    n_load = rb + SUBLANES + CONV_HALO

    def lane_group(lg, carry):
        l0 = pl.multiple_of(lg * LANES, LANES)
        w = w_ref[:, pl.ds(l0, LANES)]
        bias = b_ref[:, pl.ds(l0, LANES)]

        def row_block(r, carry2):
            r0 = pl.multiple_of(r * rb, rb)
            e = ext_sc[pl.ds(r0, n_load), pl.ds(l0, LANES)]
            y = jnp.broadcast_to(bias, (rb, LANES))
            for s in range(SUBLANES):
                q = None
                for k in range((CONV_HALO + SUBLANES) // SUBLANES):
                    j = s + SUBLANES * k - d0
                    if 0 <= j < CONV_KERNEL:
                        term = w[j:j + 1, :] * e[SUBLANES * k:SUBLANES * k + rb + SUBLANES, :]
                        q = term if q is None else q + term
                y = y + q[s:s + rb, :]
            y_sc[pl.ds(r0, rb), pl.ds(l0, LANES)] = y
            return carry2

        return lax.fori_loop(0, tm // rb, row_block, carry)

    lax.fori_loop(0, c // LANES, lane_group, 0)

    def norm_block(r, carry):
        r0 = pl.multiple_of(r * CONV_NB, CONV_NB)
        y = y_sc[pl.ds(r0, CONV_NB), :]
        mu = jnp.mean(y, axis=-1, keepdims=True)
        var = jnp.mean(jnp.square(y - mu), axis=-1, keepdims=True)
        z = (y - mu) * lax.rsqrt(var + NORM_EPS) * lng_ref[...] + lnb_ref[...]
        z = z * jax.nn.sigmoid(z)
        o_ref[pl.ds(r0, CONV_NB), :] = (z * merge_ref[...]).astype(o_ref.dtype)
        return carry

    lax.fori_loop(0, tm // CONV_NB, norm_block, 0, unroll=4)


def _conv_branch(g, first_rows, conv_w, conv_b, ln_g, ln_b, merge, tm):
    s, c = g.shape
    assert tm % CONV_HALO == 0 and tm % CONV_RB == 0 and CONV_HALO >= CONV_KERNEL - 1
    halo_blocks = tm // CONV_HALO
    row = lambda: pl.BlockSpec((1, c), lambda i: (0, 0))
    return pl.pallas_call(
        functools.partial(_conv_kernel, tm=tm, c=c),
        grid=(s // tm,),
        in_specs=[pl.BlockSpec((tm, c), lambda i: (i, 0)),
                  pl.BlockSpec((CONV_HALO, c), lambda i: (jnp.maximum(i * halo_blocks - 1, 0), 0)),
                  pl.BlockSpec((CONV_HALO, c), lambda i: (0, 0)),
                  pl.BlockSpec((CONV_KERNEL, c), lambda i: (0, 0)),
                  row(), row(), row(),
                  pl.BlockSpec((1, c), lambda i: (0, 1))],
        out_specs=pl.BlockSpec((tm, c), lambda i: (i, 0)),
        out_shape=jax.ShapeDtypeStruct((s, c), BF16),
        scratch_shapes=[pltpu.VMEM((CONV_HALO + tm + SUBLANES, c), F32),
                        pltpu.VMEM((tm, c), F32)],
        compiler_params=_params("parallel"),
        name="conv_branch",
    )(g, g, first_rows, conv_w, conv_b.reshape(1, c), ln_g.reshape(1, c), ln_b.reshape(1, c), merge)


def kernel(x, meta_tokens, rel_bias, norm_mix, w_in, lambda_q1, lambda_k1, lambda_q2, lambda_k2,
           subln_w, conv_w, conv_b, conv_ln_g, conv_ln_b, merge_scale, w_out, norm_mlp, w_up,
           w_down, norm_final):
    batch, seq, d = x.shape
    assert batch == 1 and w_in.shape[0] == 1, "single sequence, single layer"
    attn_w = N_DIFF_HEADS * DV
    conv_c = conv_w.shape[-1]
    assert w_in.shape[-1] == 3 * attn_w + 2 * conv_c and meta_tokens.shape[0] == N_META
    d_ff = w_up.shape[-1]

    tm = min(1024, seq)
    x2 = x[0]
    w_in_b = w_in[0].astype(BF16)
    w_out_b = w_out[0].astype(BF16)
    w_up_b = w_up[0].astype(BF16)
    w_down_b = w_down[0].astype(BF16)
    merge = merge_scale[0].reshape(1, attn_w + conv_c)

    xn = _rmsnorm(x2, norm_mix[0], BF16, rows=256)
    xn_meta = _rmsnorm(meta_tokens, norm_mix[0], BF16, rows=N_META)
    col_scale = jnp.concatenate([jnp.full((1, attn_w), EXP2_SCALE, F32), jnp.ones((1, 2 * attn_w), F32)], axis=1)
    qkv = _qkv_proj(xn, w_in_b, col_scale, 3 * attn_w, tm, 1024)
    qkv_meta = _qkv_proj(xn_meta, w_in_b, col_scale, 3 * attn_w, N_META, 1024)
    glu = _glu_proj(xn, w_in_b, 3 * attn_w, conv_c, tm, 512)
    glu_meta = _glu_proj(xn_meta, w_in_b, 3 * attn_w, conv_c, N_META, 512)

    near, meta_bias = _bias_tiles(rel_bias, ATTN_TK, ATTN_TK)
    qkv_meta_pad = jnp.pad(qkv_meta, ((0, META_PAD - N_META), (0, 0)))
    lams = [v[0].reshape(1, DK) for v in (lambda_q1, lambda_k1, lambda_q2, lambda_k2)]
    attn = _attention(qkv, qkv_meta_pad, near, meta_bias, lams, subln_w[0].reshape(1, DV), merge,
                      ATTN_TQ, ATTN_TK)

    first_rows = jnp.pad(glu_meta, ((CONV_HALO - N_META, 0), (0, 0)))
    conv = _conv_branch(glu, first_rows, conv_w[0], conv_b[0], conv_ln_g[0], conv_ln_b[0], merge,
                        min(512, seq))

    h1 = _out_proj(attn, conv, w_out_b, x2, tm, 1024)
    hn = _rmsnorm(h1, norm_mlp[0], BF16, rows=256)
    hid = _mlp_up(hn, w_up_b, tm, 1024)
    h2 = _mlp_down(hid, w_down_b, h1, tm, 1024, min(4096, d_ff))
    out = _rmsnorm(h2, norm_final, F32, rows=256)
    return out[None]
```
